```python
import math
import jax, jax.numpy as jnp
from jax import lax
import numpy as np

D_MODEL = 1024
BATCH = 32
SEQ = 2048
DEPTH = 1
DEC_BATCH = 128
DEC_SEQ = 4
PAST_LEN = 8192
PAGE_SIZE = 128

GLA_HEADS = 4
GLA_DK = 64
GLA_DV = 128
GLA_RANK = 16
GLA_TAU = 16.0
GLA_CHUNK = 64
GLA_QK_W = GLA_HEADS * GLA_DK
GLA_V_W = GLA_HEADS * GLA_DV
MOBA_HEADS = 8
MOBA_DH = 64
MOBA_W = MOBA_HEADS * MOBA_DH
MOBA_BLOCK = 256
MOBA_TOPK = 3
MOBA_QCHUNK = 128
NEG = -1e30
PEER_HEADS = 8
PEER_NKEYS = 128
PEER_N = PEER_NKEYS * PEER_NKEYS
PEER_QDIM = 256
PEER_HALF = PEER_QDIM // 2
PEER_TOPK = 16
PEER_TOKBLK = 256
EPS = 1e-6
IN_SIZES = (GLA_QK_W, GLA_QK_W, GLA_V_W, GLA_V_W, GLA_RANK, MOBA_W, MOBA_W, MOBA_W, D_MODEL, D_MODEL)
IN_WIDTH = sum(IN_SIZES)

kernel_name = 'gla_moba_peer_hybrid_step'


def rmsnorm(x, g):
    xf = x.astype(jnp.float32)
    y = xf * lax.rsqrt(jnp.mean(xf * xf, axis=-1, keepdims=True) + EPS)
    return (y * g.astype(jnp.float32)).astype(x.dtype)


def alibi_slopes():
    return jnp.exp2(-8.0 * jnp.arange(1, MOBA_HEADS + 1, dtype=jnp.float32) / MOBA_HEADS)


def gla_recurrence(q, k, v, log_a, s0, chunk):
    B, T, H, _ = q.shape
    n = T // chunk

    def to_chunks(z):
        return jnp.moveaxis(z.astype(jnp.float32).reshape(B, n, chunk, H, z.shape[-1]), 1, 0)

    qc, kc, vc, ac = to_chunks(q), to_chunks(k), to_chunks(v), to_chunks(log_a)
    causal = jnp.tril(jnp.ones((chunk, chunk), dtype=bool))

    def step(s, inp):
        qi, ki, vi, ai = inp
        b = jnp.cumsum(ai, axis=1)
        b_last = b[:, -1]
        q_dec = qi * jnp.exp(b)
        k_inv = ki * jnp.exp(-b)
        o_inter = jnp.einsum('bchk,bhkv->bchv', q_dec, s)
        att = jnp.where(causal, jnp.einsum('bchk,bshk->bhcs', q_dec, k_inv), 0.0)
        o_intra = jnp.einsum('bhcs,bshv->bchv', att, vi)
        k_up = ki * jnp.exp(b_last[:, None] - b)
        s_new = jnp.exp(b_last)[..., None] * s + jnp.einsum('bchk,bchv->bhkv', k_up, vi)
        return s_new, o_inter + o_intra

    s_fin, o = lax.scan(step, s0.astype(jnp.float32), (qc, kc, vc, ac))
    o = jnp.moveaxis(o, 0, 1).reshape(B, T, H, v.shape[-1])
    return o, s_fin


def moba_sequence(q, k, v, q_pos0, slopes):
    T, H, D = q.shape
    L = k.shape[0]
    nb = -(-L // MOBA_BLOCK)
    pad = nb * MOBA_BLOCK - L
    kb = jnp.pad(k, ((0, pad), (0, 0), (0, 0))).reshape(nb, MOBA_BLOCK, H, D).transpose(2, 0, 1, 3)
    vb = jnp.pad(v, ((0, pad), (0, 0), (0, 0))).reshape(nb, MOBA_BLOCK, H, D).transpose(2, 0, 1, 3)
    kmean = jnp.mean(kb.astype(jnp.float32), axis=2)
    n_sel = min(MOBA_TOPK, nb)
    qc_size = math.gcd(MOBA_QCHUNK, T)
    n_qc = T // qc_size
    scale = MOBA_DH ** -0.5
    head_ix = jnp.arange(H)[None, :, None]
    blk_ar = jnp.arange(MOBA_BLOCK)

    def attend(args):
        qi, c0 = args
        pos = q_pos0 + c0 + jnp.arange(qc_size)
        cur = pos // MOBA_BLOCK
        own = cur[0]
        qf = qi.astype(jnp.float32)
        gate = jnp.einsum('chd,hbd->chb', qf, kmean)
        past = jnp.arange(nb)[None, None, :] < cur[:, None, None]
        _, sel = lax.top_k(jnp.where(past, gate, NEG), n_sel)
        sel_ok = jnp.arange(n_sel)[None, None, :] < cur[:, None, None]
        ks = kb[head_ix, sel]
        vs = vb[head_ix, sel]
        kpos_sel = sel[..., None] * MOBA_BLOCK + blk_ar
        s_sel = jnp.einsum('chd,chnkd->chnk', qf, ks.astype(jnp.float32)) * scale
        s_sel = s_sel - slopes[None, :, None, None] * (pos[:, None, None, None] - kpos_sel)
        s_sel = jnp.where(sel_ok[..., None], s_sel, NEG).reshape(qc_size, H, n_sel * MOBA_BLOCK)
        k_own = lax.dynamic_index_in_dim(kb, own, axis=1, keepdims=False)
        v_own = lax.dynamic_index_in_dim(vb, own, axis=1, keepdims=False)
        kpos_own = own * MOBA_BLOCK + blk_ar
        s_own = jnp.einsum('chd,hkd->chk', qf, k_own.astype(jnp.float32)) * scale
        s_own = s_own - slopes[None, :, None] * (pos[:, None, None] - kpos_own[None, None, :])
        s_own = jnp.where(kpos_own[None, None, :] <= pos[:, None, None], s_own, NEG)
        p = jax.nn.softmax(jnp.concatenate([s_sel, s_own], axis=-1), axis=-1)
        p_sel = p[..., :n_sel * MOBA_BLOCK].reshape(qc_size, H, n_sel, MOBA_BLOCK)
        p_own = p[..., n_sel * MOBA_BLOCK:]
        o = (jnp.einsum('chnk,chnkd->chd', p_sel, vs.astype(jnp.float32))
             + jnp.einsum('chk,hkd->chd', p_own, v_own.astype(jnp.float32)))
        return o.astype(q.dtype)

    o = lax.map(attend, (q.reshape(n_qc, qc_size, H, D), jnp.arange(n_qc) * qc_size))
    return o.reshape(T, H, D)


def moba_prompt(q, k, v, slopes):
    return lax.map(lambda a: moba_sequence(a[0], a[1], a[2], 0, slopes), (q, k, v))


def moba_sample(q, k, v, cache_k, cache_v, page_table, layer, slopes):
    def one(args):
        qs, ks, vs, pages = args
        kp = cache_k[layer, pages]
        vp = cache_v[layer, pages]
        past = kp.shape[0] * kp.shape[1]
        k_all = jnp.concatenate([kp.reshape(past, MOBA_HEADS, MOBA_DH), ks.astype(kp.dtype)], axis=0)
        v_all = jnp.concatenate([vp.reshape(past, MOBA_HEADS, MOBA_DH), vs.astype(vp.dtype)], axis=0)
        return moba_sequence(qs, k_all, v_all, past, slopes)
    return lax.map(one, (q, k, v, page_table))


def mixer_block(x, gla_s0, moba_fn, n1, w_in, wa2, ba, gla_g, w_pa, w_pb, w_out):
    B, T, _ = x.shape
    xn = rmsnorm(x, n1)
    proj = xn @ w_in
    q_a, k_a, v_a, r_a, lr_a, q_b, k_b, v_b, g_a, g_b = jnp.split(
        proj, [int(c) for c in np.cumsum(IN_SIZES)[:-1]], axis=-1)
    log_a = jax.nn.log_sigmoid((lr_a @ wa2 + ba).astype(jnp.float32)) / GLA_TAU
    q_a = q_a.reshape(B, T, GLA_HEADS, GLA_DK) * (GLA_DK ** -0.5)
    k_a = k_a.reshape(B, T, GLA_HEADS, GLA_DK)
    v_a = v_a.reshape(B, T, GLA_HEADS, GLA_DV)
    log_a = log_a.reshape(B, T, GLA_HEADS, GLA_DK)
    o_a, s_a = gla_recurrence(q_a, k_a, v_a, log_a, gla_s0, math.gcd(GLA_CHUNK, T))
    o_a = rmsnorm(o_a.astype(x.dtype), gla_g).reshape(B, T, GLA_V_W) * jax.nn.silu(r_a)
    kb = k_b.reshape(B, T, MOBA_HEADS, MOBA_DH)
    vb = v_b.reshape(B, T, MOBA_HEADS, MOBA_DH)
    o_b = moba_fn(q_b.reshape(B, T, MOBA_HEADS, MOBA_DH), kb, vb).reshape(B, T, MOBA_W)
    merged = jax.nn.sigmoid(g_a) * (o_a @ w_pa) + jax.nn.sigmoid(g_b) * (o_b @ w_pb)
    return x + merged @ w_out, kb, vb, s_a


def peer_ffn(x2, wq, keys, u, v):
    n, D = x2.shape
    n_blk = -(-n // PEER_TOKBLK)
    xp = jnp.pad(x2, ((0, n_blk * PEER_TOKBLK - n), (0, 0))).reshape(n_blk, PEER_TOKBLK, D)

    def block(xb):
        qf = (xb @ wq).astype(jnp.float32).reshape(PEER_TOKBLK, PEER_HEADS, 2, PEER_HALF)
        s = jnp.einsum('thpd,hpnd->thpn', qf, keys.astype(jnp.float32))
        v1, i1 = lax.top_k(s[:, :, 0], PEER_TOPK)
        v2, i2 = lax.top_k(s[:, :, 1], PEER_TOPK)
        cand = (v1[..., :, None] + v2[..., None, :]).reshape(PEER_TOKBLK, PEER_HEADS, PEER_TOPK * PEER_TOPK)
        best, ci = lax.top_k(cand, PEER_TOPK)
        ids = (jnp.take_along_axis(i1, ci // PEER_TOPK, axis=-1) * PEER_NKEYS
               + jnp.take_along_axis(i2, ci % PEER_TOPK, axis=-1))
        g = jax.nn.softmax(best, axis=-1)
        ue = jnp.take(u, ids, axis=0)
        ve = jnp.take(v, ids, axis=0)
        h = jnp.einsum('td,thkd->thk', xb, ue, preferred_element_type=jnp.float32)
        act = (g * jax.nn.gelu(h, approximate=False)).astype(ve.dtype)
        out = jnp.einsum('thk,thkd->td', act, ve, preferred_element_type=jnp.float32)
        return out.astype(xb.dtype)

    return lax.map(block, xp).reshape(n_blk * PEER_TOKBLK, D)[:n]


def peer_block(x, n2, wq, keys, u, v):
    B, T, D = x.shape
    return x + peer_ffn(rmsnorm(x, n2).reshape(B * T, D), wq, keys, u, v).reshape(B, T, D)


def setup_inputs(seed: int = 0) -> dict:
    key = jax.random.key(seed)
    ks = jax.random.split(key, 20)
    n_pages = PAST_LEN // PAGE_SIZE
    n_used = DEC_BATCH * n_pages
    n_pool = (5 * n_used) // 4
    f32 = jnp.float32
    nrm = lambda k, shape, s: jax.random.normal(k, shape, f32) * s
    page_table = jax.random.permutation(ks[0], n_pool)[:n_used].reshape(DEC_BATCH, n_pages).astype(jnp.int32)
    return {
        'x_prompt': nrm(ks[1], (BATCH, SEQ, D_MODEL), 1.0),
        'x_sample': nrm(ks[2], (DEC_BATCH, DEC_SEQ, D_MODEL), 1.0),
        'cache_k': nrm(ks[3], (DEPTH, n_pool, PAGE_SIZE, MOBA_HEADS, MOBA_DH), 1.0),
        'cache_v': nrm(ks[4], (DEPTH, n_pool, PAGE_SIZE, MOBA_HEADS, MOBA_DH), 1.0),
        'state_gla': nrm(ks[5], (DEPTH, DEC_BATCH, GLA_HEADS, GLA_DK, GLA_DV), 0.3),
        'page_table': page_table,
        'norm1_g': 1.0 + nrm(ks[6], (DEPTH, D_MODEL), 0.02),
        'w_in': nrm(ks[7], (DEPTH, D_MODEL, IN_WIDTH), D_MODEL ** -0.5),
        'gla_wa2': nrm(ks[8], (DEPTH, GLA_RANK, GLA_QK_W), GLA_RANK ** -0.5),
        'gla_ba': nrm(ks[9], (DEPTH, GLA_QK_W), 0.02),
        'gla_norm_g': 1.0 + nrm(ks[10], (DEPTH, GLA_DV), 0.02),
        'w_pa': nrm(ks[11], (DEPTH, GLA_V_W, D_MODEL), GLA_V_W ** -0.5),
        'w_pb': nrm(ks[12], (DEPTH, MOBA_W, D_MODEL), MOBA_W ** -0.5),
        'w_out': nrm(ks[13], (DEPTH, D_MODEL, D_MODEL), D_MODEL ** -0.5),
        'norm2_g': 1.0 + nrm(ks[14], (DEPTH, D_MODEL), 0.02),
        'peer_wq': nrm(ks[15], (DEPTH, D_MODEL, PEER_HEADS * PEER_QDIM), D_MODEL ** -0.5),
        'peer_keys': nrm(ks[16], (DEPTH, PEER_HEADS, 2, PEER_NKEYS, PEER_HALF), PEER_HALF ** -0.5),
        'peer_u': nrm(ks[17], (DEPTH, PEER_N, D_MODEL), D_MODEL ** -0.5),
        'peer_v': nrm(ks[18], (DEPTH, PEER_N, D_MODEL), PEER_HEADS ** -0.5),
        'final_g': 1.0 + nrm(ks[19], (D_MODEL,), 0.02),
    }


def reference(x_prompt, x_sample, cache_k, cache_v, state_gla, page_table, norm1_g, w_in, gla_wa2, gla_ba,
              gla_norm_g, w_pa, w_pb, w_out, norm2_g, peer_wq, peer_keys, peer_u, peer_v, final_g):
    slopes = alibi_slopes()
    xp, xs = x_prompt, x_sample
    kp_l, vp_l, sp_l, ks_l, vs_l, ss_l = [], [], [], [], [], []
    for l in range(DEPTH):
        mix_w = (norm1_g[l], w_in[l], gla_wa2[l], gla_ba[l], gla_norm_g[l], w_pa[l], w_pb[l], w_out[l])
        s0p = jnp.zeros((xp.shape[0], GLA_HEADS, GLA_DK, GLA_DV), jnp.float32)
        xp, kp, vp, sp = mixer_block(xp, s0p, lambda q, k, v: moba_prompt(q, k, v, slopes), *mix_w)
        xp = peer_block(xp, norm2_g[l], peer_wq[l], peer_keys[l], peer_u[l], peer_v[l])
        xs, kn, vn, sn = mixer_block(
            xs, state_gla[l],
            lambda q, k, v, l=l: moba_sample(q, k, v, cache_k, cache_v, page_table, l, slopes), *mix_w)
        xs = peer_block(xs, norm2_g[l], peer_wq[l], peer_keys[l], peer_u[l], peer_v[l])
        kp_l.append(kp); vp_l.append(vp); sp_l.append(sp.astype(x_prompt.dtype))
        ks_l.append(kn.astype(cache_k.dtype)); vs_l.append(vn.astype(cache_v.dtype)); ss_l.append(sn.astype(state_gla.dtype))
    y_prompt = rmsnorm(xp, final_g)
    y_sample = rmsnorm(xs, final_g)
    k_prompt = jnp.stack(kp_l, axis=0)
    v_prompt = jnp.stack(vp_l, axis=0)
    gla_prompt = jnp.stack(sp_l, axis=0)
    k_sample = jnp.stack(ks_l, axis=0)
    v_sample = jnp.stack(vs_l, axis=0)
    gla_sample = jnp.stack(ss_l, axis=0)
    return (y_prompt, y_sample, k_prompt, v_prompt, gla_prompt, k_sample, v_sample, gla_sample)
```

```python
import functools
import math

import jax
import jax.numpy as jnp
from jax import lax
from jax.experimental import pallas as pl
from jax.experimental.pallas import tpu as pltpu

F32 = jnp.float32
BF16 = jnp.bfloat16

D_MODEL = 1024
GLA_HEADS = 4
GLA_DK = 64
GLA_DV = 128
GLA_RANK = 16
GLA_TAU = 16.0
GLA_CHUNK = 64
GLA_QK_W = GLA_HEADS * GLA_DK
GLA_V_W = GLA_HEADS * GLA_DV
MOBA_HEADS = 8
MOBA_DH = 64
MOBA_W = MOBA_HEADS * MOBA_DH
MOBA_BLOCK = 256
MOBA_TOPK = 3
MOBA_QCHUNK = 128
NEG = -1e30
PEER_HEADS = 8
PEER_NKEYS = 128
PEER_N = PEER_NKEYS * PEER_NKEYS
PEER_QDIM = 256
PEER_HALF = PEER_QDIM // 2
PEER_TOPK = 16
EPS = 1e-6
IN_SIZES = (GLA_QK_W, GLA_QK_W, GLA_V_W, GLA_V_W, GLA_RANK, MOBA_W, MOBA_W, MOBA_W, D_MODEL, D_MODEL)

LANES = 128
SUBLANES = 8
PEER_EXPERT_BLOCK = SUBLANES * PEER_NKEYS
VMEM_LIMIT = 56 * 1024 * 1024

_NT = (((1,), (1,)), ((), ()))
_TN = (((0,), (0,)), ((), ()))


def _cparams(sem):
    return pltpu.CompilerParams(dimension_semantics=sem, vmem_limit_bytes=VMEM_LIMIT)


def _rms(x, g):
    return x * lax.rsqrt(jnp.mean(x * x, axis=-1, keepdims=True) + EPS) * g


def _inproj_kernel(x_ref, g_ref, wqa, wka, wva, wra, wlr, wa2, ba, wqb, wkb, wvb, wga, wgb,
                   qa, ka, va, ra, la, qb, kb, vb, ga, gb):
    xn = _rms(x_ref[...], g_ref[...]).astype(BF16)
    for w, o in ((wqa, qa), (wka, ka), (wva, va), (wra, ra), (wqb, qb), (wkb, kb), (wvb, vb),
                 (wga, ga), (wgb, gb)):
        o[...] = jnp.dot(xn, w[...], preferred_element_type=F32).astype(o.dtype)
    lr = jnp.dot(xn, wlr[...], preferred_element_type=F32).astype(BF16)
    z = jnp.dot(lr, wa2[...], preferred_element_type=F32) + ba[...]
    la[...] = (jnp.minimum(z, 0.0) - jnp.log1p(jnp.exp(-jnp.abs(z)))) * (1.0 / GLA_TAU)


def _in_proj(x2, n1, w_in, wa2, ba, tm):
    n = x2.shape[0]
    offs = [0]
    for c in IN_SIZES:
        offs.append(offs[-1] + c)
    wb = w_in.astype(BF16)
    piece = lambda i: wb[:, offs[i]:offs[i + 1]]
    wlr = jnp.pad(piece(4), ((0, 0), (0, LANES - GLA_RANK)))
    wa2p = jnp.pad(wa2.astype(BF16), ((0, LANES - GLA_RANK), (0, 0)))
    weights = [piece(0), piece(1), piece(2), piece(3), wlr, wa2p, ba.reshape(1, -1),
               piece(5), piece(6), piece(7), piece(8), piece(9)]
    widths = [GLA_QK_W, GLA_QK_W, GLA_V_W, GLA_V_W, GLA_QK_W, MOBA_W, MOBA_W, MOBA_W, D_MODEL, D_MODEL]
    row = lambda w: pl.BlockSpec((tm, w), lambda i: (i, 0))
    full = lambda a: pl.BlockSpec(a.shape, lambda i: (0, 0))
    return pl.pallas_call(
        _inproj_kernel,
        grid=(n // tm,),
        in_specs=[row(D_MODEL), full(n1.reshape(1, -1))] + [full(w) for w in weights],
        out_specs=[row(w) for w in widths],
        out_shape=[jax.ShapeDtypeStruct((n, w), F32) for w in widths],
        compiler_params=_cparams(("parallel",)),
        name="in_proj",
    )(x2, n1.reshape(1, -1), *weights)


def _gla_kernel(qa, ka, va, la, ra, s0, g_ref, oa, sfin, s_scr, *, chunk, n_chunks):
    t = pl.program_id(1)

    @pl.when(t == 0)
    def _():
        s_scr[...] = s0[0]

    ri = lax.broadcasted_iota(jnp.int32, (chunk, chunk), 0)
    ci = lax.broadcasted_iota(jnp.int32, (chunk, chunk), 1)
    causal = ci <= ri
    tril = causal.astype(F32)
    eye_k = (lax.broadcasted_iota(jnp.int32, (GLA_DK, GLA_DK), 0)
             == lax.broadcasted_iota(jnp.int32, (GLA_DK, GLA_DK), 1))
    ones_kv = jnp.ones((GLA_DK, GLA_DV), F32)
    g = g_ref[...]

    def body(c, carry):
        r0 = pl.multiple_of(c * chunk, chunk)
        rows = pl.ds(r0, chunk)
        a = la[0, rows, :]
        b = jnp.dot(tril, a, preferred_element_type=F32, precision=lax.Precision.HIGHEST)
        b_last = b[chunk - 1:chunk, :]
        q = qa[0, rows, :] * (GLA_DK ** -0.5)
        k = ka[0, rows, :]
        q_dec = (q * jnp.exp(b)).astype(BF16)
        k_inv = (k * jnp.exp(-b)).astype(BF16)
        k_up = (k * jnp.exp(b_last - b)).astype(BF16)
        dec = jnp.exp(b_last)
        outs = []
        for h in range(GLA_HEADS):
            ks = slice(h * GLA_DK, (h + 1) * GLA_DK)
            vs = slice(h * GLA_DV, (h + 1) * GLA_DV)
            s = s_scr[h]
            v = va[0, rows, vs].astype(BF16)
            o = jnp.dot(q_dec[:, ks], s.astype(BF16), preferred_element_type=F32)
            att = lax.dot_general(q_dec[:, ks], k_inv[:, ks], _NT, preferred_element_type=F32)
            att = jnp.where(causal, att, 0.0).astype(BF16)
            o = o + jnp.dot(att, v, preferred_element_type=F32)
            kv = lax.dot_general(k_up[:, ks], v, _TN, preferred_element_type=F32)
            dmat = jnp.where(eye_k, jnp.broadcast_to(dec[:, ks], (GLA_DK, GLA_DK)), 0.0)
            dcol = jnp.dot(dmat, ones_kv, preferred_element_type=F32, precision=lax.Precision.HIGHEST)
            s_scr[h] = dcol * s + kv
            on = _rms(o, g)
            r = ra[0, rows, vs]
            outs.append(on * (r * jax.nn.sigmoid(r)))
        oa[0, rows, :] = jnp.concatenate(outs, axis=-1).astype(oa.dtype)
        return carry

    lax.fori_loop(0, n_chunks, body, 0)

    @pl.when(t == pl.num_programs(1) - 1)
    def _():
        sfin[0] = s_scr[...]


GLA_MIN_CHUNK = 16


def _gla(qa, ka, va, la, ra, s0, gla_g):
    bsz, t_real, _ = qa.shape
    chunk = math.gcd(GLA_CHUNK, t_real)
    if chunk < GLA_MIN_CHUNK:
        chunk = GLA_MIN_CHUNK
        pad = (-t_real) % chunk
        qa, ka, va, la, ra = (jnp.pad(a, ((0, 0), (0, pad), (0, 0))) for a in (qa, ka, va, la, ra))
    t = qa.shape[1]
    tb = min(t, 512)
    blk = lambda w: pl.BlockSpec((1, tb, w), lambda b, i: (b, i, 0))
    st = pl.BlockSpec((1, GLA_HEADS, GLA_DK, GLA_DV), lambda b, i: (b, 0, 0, 0))
    oa, s_fin = pl.pallas_call(
        functools.partial(_gla_kernel, chunk=chunk, n_chunks=tb // chunk),
        grid=(bsz, t // tb),
        in_specs=[blk(GLA_QK_W), blk(GLA_QK_W), blk(GLA_V_W), blk(GLA_QK_W), blk(GLA_V_W), st,
                  pl.BlockSpec((1, GLA_DV), lambda b, i: (0, 0))],
        out_specs=[blk(GLA_V_W), st],
        out_shape=[jax.ShapeDtypeStruct((bsz, t, GLA_V_W), BF16),
                   jax.ShapeDtypeStruct((bsz, GLA_HEADS, GLA_DK, GLA_DV), F32)],
        scratch_shapes=[pltpu.VMEM((GLA_HEADS, GLA_DK, GLA_DV), F32)],
        compiler_params=_cparams(("parallel", "arbitrary")),
        name="gla",
    )(qa, ka, va, la, ra, s0, gla_g.reshape(1, -1))
    return oa[:, :t_real], s_fin


def _alibi_slopes():
    return jnp.exp2(-8.0 * jnp.arange(1, MOBA_HEADS + 1, dtype=F32) / MOBA_HEADS)


def _topk_mask(gm, n_blocks):
    lane = lax.broadcasted_iota(jnp.int32, gm.shape, 1)
    rank = jnp.zeros(gm.shape, F32)
    for j in range(n_blocks):
        col = gm[:, j:j + 1]
        ahead = (col > gm) | ((col == gm) & (j < lane))
        rank = rank + jnp.where(ahead, 1.0, 0.0)
    return rank < float(MOBA_TOPK)


def _moba_prompt_kernel(slopes_ref, q_ref, k_ref, v_ref, o_ref, *, seq, heads_per_step):
    nb = seq // MOBA_BLOCK
    n_qc = seq // MOBA_QCHUNK
    scale = MOBA_DH ** -0.5
    hp = pl.program_id(1)
    rel0 = (lax.broadcasted_iota(jnp.int32, (MOBA_QCHUNK, MOBA_BLOCK), 0)
            - lax.broadcasted_iota(jnp.int32, (MOBA_QCHUNK, MOBA_BLOCK), 1))
    lane_b = lax.broadcasted_iota(jnp.int32, (MOBA_QCHUNK, LANES), 1)

    for hh in range(heads_per_step):
        cs = slice(hh * MOBA_DH, (hh + 1) * MOBA_DH)
        slope = slopes_ref[hp * heads_per_step + hh]
        kmean = jnp.concatenate(
            [jnp.mean(k_ref[0, j * MOBA_BLOCK:(j + 1) * MOBA_BLOCK, cs], axis=0, keepdims=True)
             for j in range(nb)] + [jnp.zeros((LANES - nb, MOBA_DH), F32)], axis=0)

        def q_chunk(c, carry):
            r0 = pl.multiple_of(c * MOBA_QCHUNK, MOBA_QCHUNK)
            cur = r0 // MOBA_BLOCK
            qf = q_ref[0, pl.ds(r0, MOBA_QCHUNK), cs]
            qb = qf.astype(BF16)
            gate = lax.dot_general(qf, kmean, _NT, preferred_element_type=F32,
                                   precision=lax.Precision.HIGHEST)
            past = lane_b < cur
            sel = jnp.where(past & _topk_mask(jnp.where(past, gate, NEG), nb), 1.0, 0.0)

            def scores(j):
                k0 = pl.multiple_of(j * MOBA_BLOCK, MOBA_BLOCK)
                kb = k_ref[0, pl.ds(k0, MOBA_BLOCK), cs].astype(BF16)
                s = lax.dot_general(qb, kb, _NT, preferred_element_type=F32) * scale
                rel = rel0 + (r0 - k0)
                return s - slope * rel.astype(F32), rel, k0

            s, rel, k0 = scores(cur)
            s = jnp.where(rel >= 0, s, NEG)
            m = jnp.max(s, axis=-1, keepdims=True)
            p = jnp.exp(s - m)
            l = jnp.sum(p, axis=-1, keepdims=True)
            vb = v_ref[0, pl.ds(k0, MOBA_BLOCK), cs].astype(BF16)
            acc = jnp.dot(p.astype(BF16), vb, preferred_element_type=F32)

            def past_block(j, st):
                m, l, acc = st
                selj = jnp.max(jnp.where(lane_b == j, sel, 0.0), axis=-1, keepdims=True)
                s, _, k0 = scores(j)
                s = jnp.where(selj > 0.0, s, NEG)
                m_new = jnp.maximum(m, jnp.max(s, axis=-1, keepdims=True))
                alpha = jnp.exp(m - m_new)
                p = jnp.exp(s - m_new)
                vb = v_ref[0, pl.ds(k0, MOBA_BLOCK), cs].astype(BF16)
                acc = alpha * acc + jnp.dot(p.astype(BF16), vb, preferred_element_type=F32)
                return m_new, alpha * l + jnp.sum(p, axis=-1, keepdims=True), acc

            m, l, acc = lax.fori_loop(0, cur, past_block, (m, l, acc))
            o_ref[0, pl.ds(r0, MOBA_QCHUNK), cs] = (acc / l).astype(o_ref.dtype)
            return carry

        lax.fori_loop(0, n_qc, q_chunk, 0)


def _moba_prompt(qb, kb, vb):
    bsz, t, _ = qb.shape
    hps = LANES // MOBA_DH
    blk = pl.BlockSpec((1, t, LANES), lambda b, h: (b, 0, h))
    return pl.pallas_call(
        functools.partial(_moba_prompt_kernel, seq=t, heads_per_step=hps),
        grid=(bsz, MOBA_HEADS // hps),
        in_specs=[pl.BlockSpec(memory_space=pltpu.SMEM), blk, blk, blk],
        out_specs=blk,
        out_shape=jax.ShapeDtypeStruct((bsz, t, MOBA_W), BF16),
        compiler_params=_cparams(("parallel", "parallel")),
        name="moba_prompt",
    )(_alibi_slopes(), qb, kb, vb)


def _moba_sample_kernel(pt_ref, slopes_ref, q_ref, kn_ref, vn_ref, k0_ref, k1_ref, v0_ref, v1_ref, o_ref,
                        qbd, gate_s, m_s, l_s, o_s, *, n_new, n_past_blocks, past_len):
    del pt_ref
    j = pl.program_id(1)
    rows = n_new * MOBA_HEADS
    scale = MOBA_DH ** -0.5
    half = MOBA_BLOCK // 2
    row_i = lax.broadcasted_iota(jnp.int32, (rows, 1), 0)
    head_r = row_i % MOBA_HEADS
    tok_r = row_i // MOBA_HEADS
    slope = jnp.zeros((rows, 1), F32)
    for h in range(MOBA_HEADS):
        slope = jnp.where(head_r == h, slopes_ref[h], slope)
    pos = past_len + tok_r
    head_mask = (lax.broadcasted_iota(jnp.int32, (rows, MOBA_W), 1) // MOBA_DH
                 == lax.broadcasted_iota(jnp.int32, (rows, MOBA_W), 0) % MOBA_HEADS)
    lane_nb = lax.broadcasted_iota(jnp.int32, (rows, LANES), 1)

    @pl.when(j == 0)
    def _():
        q = q_ref[0]
        qrep = jnp.concatenate([jnp.broadcast_to(q[t:t + 1, :], (MOBA_HEADS, MOBA_W))
                                for t in range(n_new)], axis=0)
        qbd[...] = jnp.where(head_mask, qrep, 0.0)
        gate_s[...] = jnp.full(gate_s.shape, NEG, F32)
        m_s[...] = jnp.full(m_s.shape, NEG, F32)
        l_s[...] = jnp.zeros(l_s.shape, F32)

    qf = qbd[...]
    qb16 = qf.astype(BF16)
    ksum = jnp.sum(k0_ref[0], axis=0, keepdims=True) + jnp.sum(k1_ref[0], axis=0, keepdims=True)
    gate_j = jnp.sum(qf * (ksum * (1.0 / MOBA_BLOCK)), axis=-1, keepdims=True)

    def half_scores(k_ref_, off):
        s = lax.dot_general(qb16, k_ref_[0].astype(BF16), _NT, preferred_element_type=F32) * scale
        kpos = j * MOBA_BLOCK + off + lax.broadcasted_iota(jnp.int32, (rows, half), 1)
        return s - slope * (pos - kpos).astype(F32)

    s0 = half_scores(k0_ref, 0)
    s1 = half_scores(k1_ref, half)
    m = jnp.maximum(jnp.max(s0, axis=-1, keepdims=True), jnp.max(s1, axis=-1, keepdims=True))
    p0 = jnp.exp(s0 - m)
    p1 = jnp.exp(s1 - m)
    l = jnp.sum(p0, axis=-1, keepdims=True) + jnp.sum(p1, axis=-1, keepdims=True)
    o = (jnp.dot(p0.astype(BF16), v0_ref[0].astype(BF16), preferred_element_type=F32)
         + jnp.dot(p1.astype(BF16), v1_ref[0].astype(BF16), preferred_element_type=F32))
    here = lane_nb == j
    gate_s[...] = jnp.where(here, gate_j, gate_s[...])
    m_s[...] = jnp.where(here, m, m_s[...])
    l_s[...] = jnp.where(here, l, l_s[...])
    o_s[j] = o

    @pl.when(j == n_past_blocks - 1)
    def _():
        gate = gate_s[...]
        sel = _topk_mask(gate, n_past_blocks) & (lane_nb < n_past_blocks)
        kn = kn_ref[0]
        vn = vn_ref[0]
        s_own = []
        for c in range(n_new):
            sc = jnp.sum(qf * kn[c:c + 1, :], axis=-1, keepdims=True) * scale
            sc = sc - slope * (tok_r - c).astype(F32)
            s_own.append(jnp.where(tok_r >= c, sc, NEG))
        m_all = jnp.where(sel, m_s[...], NEG)
        m_tot = jnp.max(m_all, axis=-1, keepdims=True)
        for sc in s_own:
            m_tot = jnp.maximum(m_tot, sc)
        w = jnp.where(sel, jnp.exp(m_all - m_tot), 0.0)
        denom = jnp.sum(w * l_s[...], axis=-1, keepdims=True)
        num = jnp.zeros((rows, MOBA_W), F32)
        for c, sc in enumerate(s_own):
            pc = jnp.exp(sc - m_tot)
            denom = denom + pc
            num = num + pc * vn[c:c + 1, :]

        def add_block(jj, acc):
            wj = jnp.sum(jnp.where(lane_nb == jj, w, 0.0), axis=-1, keepdims=True)
            return acc + wj * o_s[jj]

        num = lax.fori_loop(0, n_past_blocks, add_block, num)
        res = jnp.where(head_mask, num / denom, 0.0)
        o_ref[0] = jnp.concatenate(
            [jnp.sum(res[t * MOBA_HEADS:(t + 1) * MOBA_HEADS, :], axis=0, keepdims=True)
             for t in range(n_new)], axis=0).astype(o_ref.dtype)


def _moba_sample(qb, kb, vb, cache_k, cache_v, page_table):
    bsz, n_new, _ = qb.shape
    n_pages = page_table.shape[1]
    page = cache_k.shape[1]
    assert 2 * page == MOBA_BLOCK and n_pages % 2 == 0
    nbp = n_pages // 2
    assert nbp <= LANES
    ck = cache_k.reshape(cache_k.shape[0], page, MOBA_W)
    cv = cache_v.reshape(cache_v.shape[0], page, MOBA_W)
    new = pl.BlockSpec((1, n_new, MOBA_W), lambda b, j, pt: (b, 0, 0))
    pg = lambda o: pl.BlockSpec((1, page, MOBA_W), lambda b, j, pt: (pt[b, 2 * j + o], 0, 0))
    rows = n_new * MOBA_HEADS
    grid_spec = pltpu.PrefetchScalarGridSpec(
        num_scalar_prefetch=1,
        grid=(bsz, nbp),
        in_specs=[pl.BlockSpec(memory_space=pltpu.SMEM), new, new, new, pg(0), pg(1), pg(0), pg(1)],
        out_specs=new,
        scratch_shapes=[pltpu.VMEM((rows, MOBA_W), F32), pltpu.VMEM((rows, LANES), F32),
                        pltpu.VMEM((rows, LANES), F32), pltpu.VMEM((rows, LANES), F32),
                        pltpu.VMEM((nbp, rows, MOBA_W), F32)],
    )
    return pl.pallas_call(
        functools.partial(_moba_sample_kernel, n_new=n_new, n_past_blocks=nbp, past_len=n_pages * page),
        grid_spec=grid_spec,
        out_shape=jax.ShapeDtypeStruct((bsz, n_new, MOBA_W), BF16),
        compiler_params=_cparams(("parallel", "arbitrary")),
        name="moba_sample",
    )(page_table, _alibi_slopes(), qb, kb, vb, ck, ck, cv, cv)


def _merge_kernel(x_ref, oa_ref, ob_ref, ga_ref, gb_ref, wpa, wpb, wout, n2, wq, keys,
                  x1_ref, xn_ref, st_ref):
    ya = jnp.dot(oa_ref[...], wpa[...], preferred_element_type=F32)
    yb = jnp.dot(ob_ref[...], wpb[...], preferred_element_type=F32)
    merged = jax.nn.sigmoid(ga_ref[...]) * ya + jax.nn.sigmoid(gb_ref[...]) * yb
    x1 = x_ref[...] + jnp.dot(merged.astype(BF16), wout[...], preferred_element_type=F32)
    x1_ref[...] = x1
    xn = _rms(x1, n2[...]).astype(BF16)
    xn_ref[...] = xn
    q = jnp.dot(xn, wq[...], preferred_element_type=F32).astype(BF16)
    for i in range(2 * PEER_HEADS):
        st_ref[i * PEER_NKEYS:(i + 1) * PEER_NKEYS, :] = lax.dot_general(
            keys[i], q[:, i * PEER_HALF:(i + 1) * PEER_HALF], _NT, preferred_element_type=F32)


def _merge(x2, oa, ob, ga, gb, w_pa, w_pb, w_out, n2, wq, keys, tm):
    n = x2.shape[0]
    ws = [w_pa.astype(BF16), w_pb.astype(BF16), w_out.astype(BF16), n2.reshape(1, -1), wq.astype(BF16),
          keys.astype(BF16).reshape(2 * PEER_HEADS, PEER_NKEYS, PEER_HALF)]
    row = lambda w: pl.BlockSpec((tm, w), lambda i: (i, 0))
    full = lambda a: pl.BlockSpec(a.shape, lambda i: (0,) * a.ndim)
    n_s = 2 * PEER_HEADS * PEER_NKEYS
    return pl.pallas_call(
        _merge_kernel,
        grid=(n // tm,),
        in_specs=[row(D_MODEL), row(GLA_V_W), row(MOBA_W), row(D_MODEL), row(D_MODEL)] + [full(w) for w in ws],
        out_specs=[row(D_MODEL), row(D_MODEL), pl.BlockSpec((n_s, tm), lambda i: (0, i))],
        out_shape=[jax.ShapeDtypeStruct((n, D_MODEL), F32), jax.ShapeDtypeStruct((n, D_MODEL), BF16),
                   jax.ShapeDtypeStruct((n_s, n), F32)],
        compiler_params=_cparams(("parallel",)),
        name="merge",
    )(x2, oa, ob, ga, gb, *ws)


def _extract_top(vals, n):
    out = []
    for i in range(n):
        m = jnp.max(vals, axis=0, keepdims=True)
        out.append(m)
        if i + 1 < n:
            vals = jnp.where(vals == m, -jnp.inf, vals)
    return out


def _peer_select_kernel(st_ref, e1_ref, e2_ref, tau_ref, *, lane_tiles):
    def head(idx, carry):
        h = idx // lane_tiles
        ls = pl.ds(pl.multiple_of((idx % lane_tiles) * LANES, LANES), LANES)
        r1 = pl.multiple_of(h * 2 * PEER_NKEYS, 2 * PEER_NKEYS)
        r2 = pl.multiple_of(h * 2 * PEER_NKEYS + PEER_NKEYS, PEER_NKEYS)
        ro = pl.multiple_of(h * PEER_NKEYS, PEER_NKEYS)
        s1 = st_ref[pl.ds(r1, PEER_NKEYS), ls]
        s2 = st_ref[pl.ds(r2, PEER_NKEYS), ls]
        v1 = _extract_top(s1, PEER_TOPK)
        v2 = _extract_top(s2, PEER_TOPK)
        v1a = jnp.concatenate(v1, axis=0)
        v2a = jnp.concatenate(v2, axis=0)
        cands = [v1[0] + v2a, v1[1] + v2a[:8], v1[2] + v2a[:8], v1[3] + v2a[:8]]
        cands += [v1[i] + v2a[:8] for i in range(4, 8)]
        cands.append(v1a[8:] + v2[0])
        cand = jnp.concatenate(cands, axis=0)
        best = _extract_top(cand, PEER_TOPK)
        z = jnp.ones_like(best[0])
        for bk in best[1:]:
            z = z + jnp.exp(bk - best[0])
        e1_ref[pl.ds(ro, PEER_NKEYS), ls] = jnp.exp(s1 - v1[0]) / z
        e2_ref[pl.ds(ro, PEER_NKEYS), ls] = jnp.exp(s2 - v2[0])
        tau_ref[pl.ds(pl.multiple_of(h * SUBLANES, SUBLANES), SUBLANES), ls] = jnp.broadcast_to(
            best[-1], (SUBLANES, LANES))
        return carry

    lax.fori_loop(0, PEER_HEADS * lane_tiles, head, 0)


def _peer_select(st, tn):
    n_s, n = st.shape
    n_e = PEER_HEADS * PEER_NKEYS
    return pl.pallas_call(
        functools.partial(_peer_select_kernel, lane_tiles=tn // LANES),
        grid=(n // tn,),
        in_specs=[pl.BlockSpec((n_s, tn), lambda i: (0, i))],
        out_specs=[pl.BlockSpec((n_e, tn), lambda i: (0, i)), pl.BlockSpec((n_e, tn), lambda i: (0, i)),
                   pl.BlockSpec((PEER_HEADS * SUBLANES, tn), lambda i: (0, i))],
        out_shape=[jax.ShapeDtypeStruct((n_e, n), F32), jax.ShapeDtypeStruct((n_e, n), F32),
                   jax.ShapeDtypeStruct((PEER_HEADS * SUBLANES, n), F32)],
        compiler_params=_cparams(("parallel",)),
        name="peer_select",
    )(st)


def _peer_dense_kernel(xn_ref, x1_ref, st_ref, e1_ref, e2_ref, tau_ref, u_ref, vt_ref, fg_ref, y_ref,
                       acc, act, *, eb, tn):
    j = pl.program_id(1)

    @pl.when(j == 0)
    def _():
        acc[...] = jnp.zeros(acc.shape, F32)

    groups = eb // PEER_NKEYS
    assert groups == SUBLANES, "one aligned 8-row load of first-key scores per expert block"
    inv_sqrt2 = 1.0 / math.sqrt(2.0)

    def lane_tile(lt, carry):
        l0 = pl.multiple_of(lt * LANES, LANES)
        ls = pl.ds(l0, LANES)
        xt = xn_ref[ls, :]

        a0 = j * SUBLANES
        s1b = [st_ref[pl.ds(pl.multiple_of(h * 2 * PEER_NKEYS + a0, SUBLANES), SUBLANES), ls]
               for h in range(PEER_HEADS)]
        e1b = [e1_ref[pl.ds(pl.multiple_of(h * PEER_NKEYS + a0, SUBLANES), SUBLANES), ls]
               for h in range(PEER_HEADS)]
        for ai in range(groups):
            es = slice(ai * PEER_NKEYS, (ai + 1) * PEER_NKEYS)
            h_t = lax.dot_general(u_ref[es, :], xt, _NT, preferred_element_type=F32)
            w = jnp.zeros((PEER_NKEYS, LANES), F32)
            for h in range(PEER_HEADS):
                s1r = s1b[h][ai:ai + 1, :]
                e1r = e1b[h][ai:ai + 1, :]
                s2 = st_ref[h * 2 * PEER_NKEYS + PEER_NKEYS:(h + 1) * 2 * PEER_NKEYS, ls]
                e2 = e2_ref[h * PEER_NKEYS:(h + 1) * PEER_NKEYS, ls]
                tau = tau_ref[h * SUBLANES:h * SUBLANES + 1, ls]
                w = w + jnp.where(s1r + s2 >= tau, e2 * e1r, 0.0)
            gelu = 0.5 * h_t * (1.0 + lax.erf(h_t * inv_sqrt2))
            act[es, ls] = (w * gelu).astype(BF16)
        return carry

    lax.fori_loop(0, tn // LANES, lane_tile, 0)
    acc[...] += jnp.dot(vt_ref[...], act[...], preferred_element_type=F32)

    @pl.when(j == pl.num_programs(1) - 1)
    def _():
        y = x1_ref[...] + acc[...].T
        y_ref[...] = _rms(y, fg_ref[...])


def _peer_dense(xn, x1, st, e1, e2, tau, u_bf, vt_bf, final_g, tn, eb):
    n = xn.shape[0]
    n_s = st.shape[0]
    n_e = e1.shape[0]
    tok = lambda w: pl.BlockSpec((tn, w), lambda i, j: (i, 0))
    col = lambda r: pl.BlockSpec((r, tn), lambda i, j: (0, i))
    return pl.pallas_call(
        functools.partial(_peer_dense_kernel, eb=eb, tn=tn),
        grid=(n // tn, PEER_N // eb),
        in_specs=[tok(D_MODEL), tok(D_MODEL), col(n_s), col(n_e), col(n_e), col(tau.shape[0]),
                  pl.BlockSpec((eb, D_MODEL), lambda i, j: (j, 0)),
                  pl.BlockSpec((D_MODEL, eb), lambda i, j: (0, j)),
                  pl.BlockSpec((1, D_MODEL), lambda i, j: (0, 0))],
        out_specs=tok(D_MODEL),
        out_shape=jax.ShapeDtypeStruct((n, D_MODEL), F32),
        scratch_shapes=[pltpu.VMEM((D_MODEL, tn), F32), pltpu.VMEM((eb, tn), BF16)],
        compiler_params=_cparams(("parallel", "arbitrary")),
        name="peer_dense",
    )(xn, x1, st, e1, e2, tau, u_bf, vt_bf, final_g.reshape(1, -1))


def _group(x, s0, moba_fn, wts, tm, tn, eb):
    (n1, w_in, wa2, ba, gla_g, w_pa, w_pb, w_out, n2, wq, keys, u_bf, vt_bf, final_g) = wts
    bsz, t, _ = x.shape
    n = bsz * t
    x2 = x.reshape(n, D_MODEL)
    qa, ka, va, ra, la, qb, kb, vb, ga, gb = _in_proj(x2, n1, w_in, wa2, ba, tm)
    r3 = lambda a: a.reshape(bsz, t, a.shape[-1])
    oa, s_fin = _gla(r3(qa), r3(ka), r3(va), r3(la), r3(ra), s0, gla_g)
    ob = moba_fn(r3(qb), r3(kb), r3(vb))
    x1, xn, st = _merge(x2, oa.reshape(n, GLA_V_W), ob.reshape(n, MOBA_W), ga, gb,
                        w_pa, w_pb, w_out, n2, wq, keys, tm)
    e1, e2, tau = _peer_select(st, tn)
    y = _peer_dense(xn, x1, st, e1, e2, tau, u_bf, vt_bf, final_g, tn, eb)
    kv_shape = (1, bsz, t, MOBA_HEADS, MOBA_DH)
    return y.reshape(bsz, t, D_MODEL), kb.reshape(kv_shape), vb.reshape(kv_shape), s_fin[None]


def kernel(x_prompt, x_sample, cache_k, cache_v, state_gla, page_table, norm1_g, w_in, gla_wa2, gla_ba,
           gla_norm_g, w_pa, w_pb, w_out, norm2_g, peer_wq, peer_keys, peer_u, peer_v, final_g):
    assert w_in.shape[0] == 1, "single-layer step"
    u_bf = peer_u[0].astype(BF16)
    vt_bf = peer_v[0].astype(BF16).T
    wts = (norm1_g[0], w_in[0], gla_wa2[0], gla_ba[0], gla_norm_g[0], w_pa[0], w_pb[0], w_out[0],
           norm2_g[0], peer_wq[0], peer_keys[0], u_bf, vt_bf, final_g)
    bp = x_prompt.shape[0]
    s0p = jnp.zeros((bp, GLA_HEADS, GLA_DK, GLA_DV), F32)
    n_s = x_sample.shape[0] * x_sample.shape[1]
    tok_blk = lambda n, pref: pref if n % pref == 0 else n
    n_p = bp * x_prompt.shape[1]
    yp, kp, vp, sp = _group(x_prompt, s0p, _moba_prompt, wts,
                            tok_blk(n_p, 256), tok_blk(n_p, 512), PEER_EXPERT_BLOCK)
    ck, cv = cache_k[0], cache_v[0]
    ys, ks, vs, ss = _group(x_sample, state_gla[0],
                            lambda q, k, v: _moba_sample(q, k, v, ck, cv, page_table), wts,
                            tok_blk(n_s, 256), tok_blk(n_s, 512), PEER_EXPERT_BLOCK)
    return (yp, ys, kp, vp, sp, ks, vs, ss)
```

```python
import functools
import math

import jax
import jax.numpy as jnp
from jax import lax
from jax.experimental import pallas as pl
from jax.experimental.pallas import tpu as pltpu

F32 = jnp.float32
BF16 = jnp.bfloat16

D_MODEL = 1024
GLA_HEADS = 4
GLA_DK = 64
GLA_DV = 128
GLA_RANK = 16
GLA_TAU = 16.0
GLA_CHUNK = 64
GLA_QK_W = GLA_HEADS * GLA_DK
GLA_V_W = GLA_HEADS * GLA_DV
MOBA_HEADS = 8
MOBA_DH = 64
MOBA_W = MOBA_HEADS * MOBA_DH
MOBA_BLOCK = 256
MOBA_TOPK = 3
MOBA_QCHUNK = 128
NEG = -1e30
PEER_HEADS = 8
PEER_NKEYS = 128
PEER_N = PEER_NKEYS * PEER_NKEYS
PEER_QDIM = 256
PEER_HALF = PEER_QDIM // 2
PEER_TOPK = 16
EPS = 1e-6
IN_SIZES = (GLA_QK_W, GLA_QK_W, GLA_V_W, GLA_V_W, GLA_RANK, MOBA_W, MOBA_W, MOBA_W, D_MODEL, D_MODEL)

LANES = 128
SUBLANES = 8
PEER_EXPERT_BLOCK = SUBLANES * PEER_NKEYS
PEER_KEY_STRIP = 32
PEER_LANE_TILES_PER_ITER = 2
VMEM_LIMIT = 56 * 1024 * 1024

_NT = (((1,), (1,)), ((), ()))
_TN = (((0,), (0,)), ((), ()))


def _cparams(sem):
    return pltpu.CompilerParams(dimension_semantics=sem, vmem_limit_bytes=VMEM_LIMIT)


def _rms(x, g):
    return x * lax.rsqrt(jnp.mean(x * x, axis=-1, keepdims=True) + EPS) * g


def _inproj_kernel(x_ref, g_ref, wqa, wka, wva, wra, wlr, wa2, ba, wqb, wkb, wvb, wga, wgb,
                   qa, ka, va, ra, la, qb, kb, vb, ga, gb):
    xn = _rms(x_ref[...], g_ref[...]).astype(BF16)
    for w, o in ((wqa, qa), (wka, ka), (wva, va), (wra, ra), (wqb, qb), (wkb, kb), (wvb, vb),
                 (wga, ga), (wgb, gb)):
        o[...] = jnp.dot(xn, w[...], preferred_element_type=F32).astype(o.dtype)
    lr = jnp.dot(xn, wlr[...], preferred_element_type=F32).astype(BF16)
    z = jnp.dot(lr, wa2[...], preferred_element_type=F32) + ba[...]
    la[...] = (jnp.minimum(z, 0.0) - jnp.log1p(jnp.exp(-jnp.abs(z)))) * (1.0 / GLA_TAU)


def _in_proj(x2, n1, w_in, wa2, ba, tm):
    n = x2.shape[0]
    offs = [0]
    for c in IN_SIZES:
        offs.append(offs[-1] + c)
    wb = w_in.astype(BF16)
    piece = lambda i: wb[:, offs[i]:offs[i + 1]]
    wlr = jnp.pad(piece(4), ((0, 0), (0, LANES - GLA_RANK)))
    wa2p = jnp.pad(wa2.astype(BF16), ((0, LANES - GLA_RANK), (0, 0)))
    weights = [piece(0), piece(1), piece(2), piece(3), wlr, wa2p, ba.reshape(1, -1),
               piece(5), piece(6), piece(7), piece(8), piece(9)]
    widths = [GLA_QK_W, GLA_QK_W, GLA_V_W, GLA_V_W, GLA_QK_W, MOBA_W, MOBA_W, MOBA_W, D_MODEL, D_MODEL]
    row = lambda w: pl.BlockSpec((tm, w), lambda i: (i, 0))
    full = lambda a: pl.BlockSpec(a.shape, lambda i: (0, 0))
    return pl.pallas_call(
        _inproj_kernel,
        grid=(n // tm,),
        in_specs=[row(D_MODEL), full(n1.reshape(1, -1))] + [full(w) for w in weights],
        out_specs=[row(w) for w in widths],
        out_shape=[jax.ShapeDtypeStruct((n, w), F32) for w in widths],
        compiler_params=_cparams(("parallel",)),
        name="in_proj",
    )(x2, n1.reshape(1, -1), *weights)


def _gla_kernel(qa, ka, va, la, ra, s0, g_ref, oa, sfin, s_scr, *, chunk, n_chunks):
    t = pl.program_id(1)

    @pl.when(t == 0)
    def _():
        s_scr[...] = s0[0]

    ri = lax.broadcasted_iota(jnp.int32, (chunk, chunk), 0)
    ci = lax.broadcasted_iota(jnp.int32, (chunk, chunk), 1)
    causal = ci <= ri
    tril = causal.astype(F32)
    eye_k = (lax.broadcasted_iota(jnp.int32, (GLA_DK, GLA_DK), 0)
             == lax.broadcasted_iota(jnp.int32, (GLA_DK, GLA_DK), 1))
    ones_kv = jnp.ones((GLA_DK, GLA_DV), F32)
    g = g_ref[...]

    def body(c, carry):
        r0 = pl.multiple_of(c * chunk, chunk)
        rows = pl.ds(r0, chunk)
        a = la[0, rows, :]
        b = jnp.dot(tril, a, preferred_element_type=F32, precision=lax.Precision.HIGHEST)
        b_last = b[chunk - 1:chunk, :]
        q = qa[0, rows, :] * (GLA_DK ** -0.5)
        k = ka[0, rows, :]
        q_dec = (q * jnp.exp(b)).astype(BF16)
        k_inv = (k * jnp.exp(-b)).astype(BF16)
        k_up = (k * jnp.exp(b_last - b)).astype(BF16)
        dec = jnp.exp(b_last)
        outs = []
        for h in range(GLA_HEADS):
            ks = slice(h * GLA_DK, (h + 1) * GLA_DK)
            vs = slice(h * GLA_DV, (h + 1) * GLA_DV)
            s = s_scr[h]
            v = va[0, rows, vs].astype(BF16)
            o = jnp.dot(q_dec[:, ks], s.astype(BF16), preferred_element_type=F32)
            att = lax.dot_general(q_dec[:, ks], k_inv[:, ks], _NT, preferred_element_type=F32)
            att = jnp.where(causal, att, 0.0).astype(BF16)
            o = o + jnp.dot(att, v, preferred_element_type=F32)
            kv = lax.dot_general(k_up[:, ks], v, _TN, preferred_element_type=F32)
            dmat = jnp.where(eye_k, jnp.broadcast_to(dec[:, ks], (GLA_DK, GLA_DK)), 0.0)
            dcol = jnp.dot(dmat, ones_kv, preferred_element_type=F32, precision=lax.Precision.HIGHEST)
            s_scr[h] = dcol * s + kv
            on = _rms(o, g)
            r = ra[0, rows, vs]
            outs.append(on * (r * jax.nn.sigmoid(r)))
        oa[0, rows, :] = jnp.concatenate(outs, axis=-1).astype(oa.dtype)
        return carry

    lax.fori_loop(0, n_chunks, body, 0)

    @pl.when(t == pl.num_programs(1) - 1)
    def _():
        sfin[0] = s_scr[...]


GLA_MIN_CHUNK = 16


def _gla(qa, ka, va, la, ra, s0, gla_g):
    bsz, t_real, _ = qa.shape
    chunk = math.gcd(GLA_CHUNK, t_real)
    if chunk < GLA_MIN_CHUNK:
        chunk = GLA_MIN_CHUNK
        pad = (-t_real) % chunk
        qa, ka, va, la, ra = (jnp.pad(a, ((0, 0), (0, pad), (0, 0))) for a in (qa, ka, va, la, ra))
    t = qa.shape[1]
    tb = min(t, 512)
    blk = lambda w: pl.BlockSpec((1, tb, w), lambda b, i: (b, i, 0))
    st = pl.BlockSpec((1, GLA_HEADS, GLA_DK, GLA_DV), lambda b, i: (b, 0, 0, 0))
    oa, s_fin = pl.pallas_call(
        functools.partial(_gla_kernel, chunk=chunk, n_chunks=tb // chunk),
        grid=(bsz, t // tb),
        in_specs=[blk(GLA_QK_W), blk(GLA_QK_W), blk(GLA_V_W), blk(GLA_QK_W), blk(GLA_V_W), st,
                  pl.BlockSpec((1, GLA_DV), lambda b, i: (0, 0))],
        out_specs=[blk(GLA_V_W), st],
        out_shape=[jax.ShapeDtypeStruct((bsz, t, GLA_V_W), BF16),
                   jax.ShapeDtypeStruct((bsz, GLA_HEADS, GLA_DK, GLA_DV), F32)],
        scratch_shapes=[pltpu.VMEM((GLA_HEADS, GLA_DK, GLA_DV), F32)],
        compiler_params=_cparams(("parallel", "arbitrary")),
        name="gla",
    )(qa, ka, va, la, ra, s0, gla_g.reshape(1, -1))
    return oa[:, :t_real], s_fin


def _alibi_slopes():
    return jnp.exp2(-8.0 * jnp.arange(1, MOBA_HEADS + 1, dtype=F32) / MOBA_HEADS)


def _topk_mask(gm, n_blocks):
    lane = lax.broadcasted_iota(jnp.int32, gm.shape, 1)
    rank = jnp.zeros(gm.shape, F32)
    for j in range(n_blocks):
        col = gm[:, j:j + 1]
        ahead = (col > gm) | ((col == gm) & (j < lane))
        rank = rank + jnp.where(ahead, 1.0, 0.0)
    return rank < float(MOBA_TOPK)


def _moba_prompt_kernel(slopes_ref, q_ref, k_ref, v_ref, o_ref, *, seq):
    nb = seq // MOBA_BLOCK
    qrows = MOBA_BLOCK
    heads = LANES // MOBA_DH
    scale = MOBA_DH ** -0.5
    hp = pl.program_id(1)
    lane_q = lax.broadcasted_iota(jnp.int32, (qrows, LANES), 1)
    rel0 = (lax.broadcasted_iota(jnp.int32, (qrows, MOBA_BLOCK), 0)
            - lax.broadcasted_iota(jnp.int32, (qrows, MOBA_BLOCK), 1))
    causal = rel0 >= 0
    e_rows = lax.broadcasted_iota(jnp.int32, (LANES, MOBA_BLOCK), 0)
    hmask = [lane_q // MOBA_DH == x for x in range(heads)]
    slopes = [slopes_ref[hp * heads + x] for x in range(heads)]
    srel = [slopes[x] * rel0.astype(F32) for x in range(heads)]
    kmean = jnp.concatenate(
        [jnp.mean(k_ref[0, j * MOBA_BLOCK:(j + 1) * MOBA_BLOCK, :], axis=0, keepdims=True)
         for j in range(nb)] + [jnp.zeros((LANES - nb, LANES), F32)], axis=0)

    def q_block(cur, carry):
        r0 = pl.multiple_of(cur * qrows, qrows)
        q = q_ref[0, pl.ds(r0, qrows), :]
        past = lane_q < cur
        qx, sel = [], []
        for x in range(heads):
            qh = jnp.where(hmask[x], q, 0.0)
            gate = lax.dot_general(qh, kmean, _NT, preferred_element_type=F32,
                                   precision=lax.Precision.HIGHEST)
            keep = past & _topk_mask(jnp.where(past, gate, NEG), nb)
            sel.append(jnp.where(keep, 1.0, 0.0).astype(BF16))
            qx.append(qh.astype(BF16))

        kb = k_ref[0, pl.ds(r0, qrows), :].astype(BF16)
        vb = v_ref[0, pl.ds(r0, qrows), :].astype(BF16)
        init = []
        for x in range(heads):
            s = lax.dot_general(qx[x], kb, _NT, preferred_element_type=F32) * scale - srel[x]
            s = jnp.where(causal, s, NEG)
            m = jnp.max(s, axis=-1, keepdims=True)
            p = jnp.exp(s - m)
            init.append((m, jnp.sum(p, axis=-1, keepdims=True),
                         jnp.dot(p.astype(BF16), vb, preferred_element_type=F32)))

        def past_block(j, st):
            k0 = pl.multiple_of(j * MOBA_BLOCK, MOBA_BLOCK)
            kb = k_ref[0, pl.ds(k0, MOBA_BLOCK), :].astype(BF16)
            vb = v_ref[0, pl.ds(k0, MOBA_BLOCK), :].astype(BF16)
            ej = jnp.where(e_rows == j, 1.0, 0.0).astype(BF16)
            off = (r0 - k0).astype(F32)
            out = []
            for x in range(heads):
                m, l, acc = st[x]
                selb = jnp.dot(sel[x], ej, preferred_element_type=F32)
                s = lax.dot_general(qx[x], kb, _NT, preferred_element_type=F32) * scale
                s = s - (srel[x] + slopes[x] * off)
                s = jnp.where(selb > 0.5, s, NEG)
                m_new = jnp.maximum(m, jnp.max(s, axis=-1, keepdims=True))
                alpha = jnp.exp(m - m_new)
                p = jnp.exp(s - m_new)
                acc = alpha * acc + jnp.dot(p.astype(BF16), vb, preferred_element_type=F32)
                out.append((m_new, alpha * l + jnp.sum(p, axis=-1, keepdims=True), acc))
            return tuple(out)

        st = lax.fori_loop(0, cur, past_block, tuple(init))
        o = st[heads - 1][2] / st[heads - 1][1]
        for x in range(heads - 2, -1, -1):
            o = jnp.where(hmask[x], st[x][2] / st[x][1], o)
        o_ref[0, pl.ds(r0, qrows), :] = o.astype(o_ref.dtype)
        return carry

    lax.fori_loop(0, nb, q_block, 0)


def _moba_prompt(qb, kb, vb):
    bsz, t, _ = qb.shape
    blk = pl.BlockSpec((1, t, LANES), lambda b, h: (b, 0, h))
    return pl.pallas_call(
        functools.partial(_moba_prompt_kernel, seq=t),
        grid=(bsz, MOBA_W // LANES),
        in_specs=[pl.BlockSpec(memory_space=pltpu.SMEM), blk, blk, blk],
        out_specs=blk,
        out_shape=jax.ShapeDtypeStruct((bsz, t, MOBA_W), BF16),
        compiler_params=_cparams(("parallel", "parallel")),
        name="moba_prompt",
    )(_alibi_slopes(), qb, kb, vb)


def _moba_sample_kernel(pt_ref, slopes_ref, q_ref, kn_ref, vn_ref, *rest, n_new, n_past_blocks, past_len,
                        blocks_per_step, pages_per_block):
    del pt_ref
    n_pg = blocks_per_step * pages_per_block
    k_refs, v_refs = rest[:n_pg], rest[n_pg:2 * n_pg]
    o_ref, gate_s, m_s, l_s, o_s = rest[2 * n_pg:]
    j = pl.program_id(1)
    rows = n_new * MOBA_HEADS
    page = k_refs[0].shape[1]
    cols = page * MOBA_HEADS
    scale = MOBA_DH ** -0.5
    row_i = lax.broadcasted_iota(jnp.int32, (rows, 1), 0)
    head_r = row_i % MOBA_HEADS
    tok_r = row_i // MOBA_HEADS
    slope = jnp.zeros((rows, 1), F32)
    for h in range(MOBA_HEADS):
        slope = jnp.where(head_r == h, slopes_ref[h], slope)
    col_i = lax.broadcasted_iota(jnp.int32, (rows, cols), 1)
    same_head = col_i % MOBA_HEADS == lax.broadcasted_iota(jnp.int32, (rows, cols), 0) % MOBA_HEADS
    key_c = (col_i // MOBA_HEADS).astype(F32)
    lane_nb = lax.broadcasted_iota(jnp.int32, (rows, LANES), 1)

    @pl.when(j == 0)
    def _():
        gate_s[...] = jnp.full(gate_s.shape, NEG, F32)
        m_s[...] = jnp.full(m_s.shape, NEG, F32)
        l_s[...] = jnp.zeros(l_s.shape, F32)

    q = q_ref[0]
    q16 = q.astype(BF16)
    for bi in range(blocks_per_step):
        blk = j * blocks_per_step + bi
        pages_k = [k_refs[bi * pages_per_block + o][0] for o in range(pages_per_block)]
        ksum = pages_k[0].sum(axis=0)
        for pk in pages_k[1:]:
            ksum = ksum + pk.sum(axis=0)
        kmean = ksum * (1.0 / MOBA_BLOCK)
        gate_j = jnp.concatenate(
            [jnp.sum(q[t * MOBA_HEADS:(t + 1) * MOBA_HEADS, :] * kmean, axis=-1, keepdims=True)
             for t in range(n_new)], axis=0)
        ss = []
        for o, pk in enumerate(pages_k):
            k2 = pk.reshape(cols, MOBA_DH).astype(BF16)
            s = lax.dot_general(q16, k2, _NT, preferred_element_type=F32) * scale
            d0 = (past_len - blk * MOBA_BLOCK - o * page).astype(F32)
            dist = (tok_r.astype(F32) + d0) - key_c
            ss.append(jnp.where(same_head, s - slope * dist, NEG))
        m = jnp.max(ss[0], axis=-1, keepdims=True)
        for s in ss[1:]:
            m = jnp.maximum(m, jnp.max(s, axis=-1, keepdims=True))
        l = jnp.zeros((rows, 1), F32)
        o_blk = jnp.zeros((rows, MOBA_DH), F32)
        for o, s in enumerate(ss):
            p = jnp.exp(s - m)
            l = l + jnp.sum(p, axis=-1, keepdims=True)
            v2 = v_refs[bi * pages_per_block + o][0].reshape(cols, MOBA_DH).astype(BF16)
            o_blk = o_blk + jnp.dot(p.astype(BF16), v2, preferred_element_type=F32)
        here = lane_nb == blk
        gate_s[...] = jnp.where(here, gate_j, gate_s[...])
        m_s[...] = jnp.where(here, m, m_s[...])
        l_s[...] = jnp.where(here, l, l_s[...])
        o_s[blk] = o_blk

    @pl.when(j == pl.num_programs(1) - 1)
    def _():
        sel = _topk_mask(gate_s[...], n_past_blocks) & (lane_nb < n_past_blocks)
        kn = kn_ref[0]
        vn = vn_ref[0]
        s_own, v_own = [], []
        for c in range(n_new):
            kc = jnp.concatenate([kn[c * MOBA_HEADS:(c + 1) * MOBA_HEADS, :]] * n_new, axis=0)
            v_own.append(jnp.concatenate([vn[c * MOBA_HEADS:(c + 1) * MOBA_HEADS, :]] * n_new, axis=0))
            sc = jnp.sum(q * kc, axis=-1, keepdims=True) * scale - slope * (tok_r - c).astype(F32)
            s_own.append(jnp.where(tok_r >= c, sc, NEG))
        m_all = jnp.where(sel, m_s[...], NEG)
        m_tot = jnp.max(m_all, axis=-1, keepdims=True)
        for sc in s_own:
            m_tot = jnp.maximum(m_tot, sc)
        w = jnp.where(sel, jnp.exp(m_all - m_tot), 0.0)
        denom = jnp.sum(w * l_s[...], axis=-1, keepdims=True)
        num = jnp.zeros((rows, MOBA_DH), F32)
        for sc, vc in zip(s_own, v_own):
            pc = jnp.exp(sc - m_tot)
            denom = denom + pc
            num = num + pc * vc

        def add_block(jj, acc):
            wj = jnp.sum(jnp.where(lane_nb == jj, w, 0.0), axis=-1, keepdims=True)
            return acc + wj * o_s[jj]

        num = lax.fori_loop(0, n_past_blocks, add_block, num)
        o_ref[0] = (num / denom).astype(o_ref.dtype)


MOBA_SAMPLE_BLOCKS_PER_STEP = 2


def _moba_sample(qb, kb, vb, cache_k, cache_v, page_table):
    bsz, n_new, _ = qb.shape
    n_pages = page_table.shape[1]
    page = cache_k.shape[1]
    ppb = MOBA_BLOCK // page
    assert ppb * page == MOBA_BLOCK and n_pages % ppb == 0
    nbp = n_pages // ppb
    bps = MOBA_SAMPLE_BLOCKS_PER_STEP if nbp % MOBA_SAMPLE_BLOCKS_PER_STEP == 0 else 1
    assert nbp <= LANES
    rows = n_new * MOBA_HEADS
    to_rows = lambda a: a.reshape(bsz, rows, MOBA_DH)
    new = pl.BlockSpec((1, rows, MOBA_DH), lambda b, j, pt: (b, 0, 0))
    n_pg = bps * ppb
    pg = lambda o: pl.BlockSpec((1, page, MOBA_HEADS, MOBA_DH),
                                lambda b, j, pt: (pt[b, n_pg * j + o], 0, 0, 0))
    grid_spec = pltpu.PrefetchScalarGridSpec(
        num_scalar_prefetch=1,
        grid=(bsz, nbp // bps),
        in_specs=[pl.BlockSpec(memory_space=pltpu.SMEM), new, new, new]
                 + [pg(o) for o in range(n_pg)] + [pg(o) for o in range(n_pg)],
        out_specs=new,
        scratch_shapes=[pltpu.VMEM((rows, LANES), F32), pltpu.VMEM((rows, LANES), F32),
                        pltpu.VMEM((rows, LANES), F32), pltpu.VMEM((nbp, rows, MOBA_DH), F32)],
    )
    out = pl.pallas_call(
        functools.partial(_moba_sample_kernel, n_new=n_new, n_past_blocks=nbp, past_len=n_pages * page,
                          blocks_per_step=bps, pages_per_block=ppb),
        grid_spec=grid_spec,
        out_shape=jax.ShapeDtypeStruct((bsz, rows, MOBA_DH), BF16),
        compiler_params=_cparams(("parallel", "arbitrary")),
        name="moba_sample",
    )(page_table, _alibi_slopes(), to_rows(qb), to_rows(kb), to_rows(vb),
      *([cache_k] * n_pg), *([cache_v] * n_pg))
    return out.reshape(bsz, n_new, MOBA_W)


def _merge_kernel(x_ref, oa_ref, ob_ref, ga_ref, gb_ref, wpa, wpb, wout, n2, wq, keys,
                  x1_ref, xnt_ref, st_ref):
    ya = jnp.dot(oa_ref[...], wpa[...], preferred_element_type=F32)
    yb = jnp.dot(ob_ref[...], wpb[...], preferred_element_type=F32)
    merged = jax.nn.sigmoid(ga_ref[...]) * ya + jax.nn.sigmoid(gb_ref[...]) * yb
    x1 = x_ref[...] + jnp.dot(merged.astype(BF16), wout[...], preferred_element_type=F32)
    x1_ref[...] = x1
    xn = _rms(x1, n2[...])
    xnt_ref[...] = xn.T.astype(BF16)
    q = jnp.dot(xn.astype(BF16), wq[...], preferred_element_type=F32).astype(BF16)
    for i in range(2 * PEER_HEADS):
        st_ref[i * PEER_NKEYS:(i + 1) * PEER_NKEYS, :] = lax.dot_general(
            keys[i], q[:, i * PEER_HALF:(i + 1) * PEER_HALF], _NT, preferred_element_type=F32)


def _merge(x2, oa, ob, ga, gb, w_pa, w_pb, w_out, n2, wq, keys, tm):
    n = x2.shape[0]
    ws = [w_pa.astype(BF16), w_pb.astype(BF16), w_out.astype(BF16), n2.reshape(1, -1), wq.astype(BF16),
          keys.astype(BF16).reshape(2 * PEER_HEADS, PEER_NKEYS, PEER_HALF)]
    row = lambda w: pl.BlockSpec((tm, w), lambda i: (i, 0))
    full = lambda a: pl.BlockSpec(a.shape, lambda i: (0,) * a.ndim)
    n_s = 2 * PEER_HEADS * PEER_NKEYS
    return pl.pallas_call(
        _merge_kernel,
        grid=(n // tm,),
        in_specs=[row(D_MODEL), row(GLA_V_W), row(MOBA_W), row(D_MODEL), row(D_MODEL)] + [full(w) for w in ws],
        out_specs=[row(D_MODEL), pl.BlockSpec((D_MODEL, tm), lambda i: (0, i)),
                   pl.BlockSpec((n_s, tm), lambda i: (0, i))],
        out_shape=[jax.ShapeDtypeStruct((n, D_MODEL), F32), jax.ShapeDtypeStruct((D_MODEL, n), BF16),
                   jax.ShapeDtypeStruct((n_s, n), F32)],
        compiler_params=_cparams(("parallel",)),
        name="merge",
    )(x2, oa, ob, ga, gb, *ws)


def _extract_top(vals, n):
    out = []
    for i in range(n):
        m = jnp.max(vals, axis=0, keepdims=True)
        out.append(m)
        if i + 1 < n:
            vals = jnp.where(vals == m, -jnp.inf, vals)
    return out


def _peer_select_kernel(st_ref, e1_ref, e2_ref, tau_ref, *, lane_tiles):
    def head(idx, carry):
        h = idx // lane_tiles
        ls = pl.ds(pl.multiple_of((idx % lane_tiles) * LANES, LANES), LANES)
        r1 = pl.multiple_of(h * 2 * PEER_NKEYS, 2 * PEER_NKEYS)
        r2 = pl.multiple_of(h * 2 * PEER_NKEYS + PEER_NKEYS, PEER_NKEYS)
        ro = pl.multiple_of(h * PEER_NKEYS, PEER_NKEYS)
        s1 = st_ref[pl.ds(r1, PEER_NKEYS), ls]
        s2 = st_ref[pl.ds(r2, PEER_NKEYS), ls]
        v1 = _extract_top(s1, PEER_TOPK)
        v2 = _extract_top(s2, PEER_TOPK)
        v1a = jnp.concatenate(v1, axis=0)
        v2a = jnp.concatenate(v2, axis=0)
        cands = [v1[0] + v2a, v1[1] + v2a[:8], v1[2] + v2a[:8], v1[3] + v2a[:8]]
        cands += [v1[i] + v2a[:8] for i in range(4, 8)]
        cands.append(v1a[8:] + v2[0])
        cand = jnp.concatenate(cands, axis=0)
        best = _extract_top(cand, PEER_TOPK)
        z = jnp.ones_like(best[0])
        for bk in best[1:]:
            z = z + jnp.exp(bk - best[0])
        e1_ref[pl.ds(ro, PEER_NKEYS), ls] = jnp.exp(s1 - v1[0]) / z
        e2_ref[pl.ds(ro, PEER_NKEYS), ls] = jnp.exp(s2 - v2[0])
        tau_ref[pl.ds(pl.multiple_of(h * SUBLANES, SUBLANES), SUBLANES), ls] = jnp.broadcast_to(
            best[-1], (SUBLANES, LANES))
        return carry

    lax.fori_loop(0, PEER_HEADS * lane_tiles, head, 0)


def _peer_select(st, tn):
    n_s, n = st.shape
    n_e = PEER_HEADS * PEER_NKEYS
    return pl.pallas_call(
        functools.partial(_peer_select_kernel, lane_tiles=tn // LANES),
        grid=(n // tn,),
        in_specs=[pl.BlockSpec((n_s, tn), lambda i: (0, i))],
        out_specs=[pl.BlockSpec((n_e, tn), lambda i: (0, i)), pl.BlockSpec((n_e, tn), lambda i: (0, i)),
                   pl.BlockSpec((PEER_HEADS * SUBLANES, tn), lambda i: (0, i))],
        out_shape=[jax.ShapeDtypeStruct((n_e, n), F32), jax.ShapeDtypeStruct((n_e, n), F32),
                   jax.ShapeDtypeStruct((PEER_HEADS * SUBLANES, n), F32)],
        compiler_params=_cparams(("parallel",)),
        name="peer_select",
    )(st)


def _peer_dense_kernel(xnt_ref, x1_ref, st_ref, e1_ref, e2_ref, tau_ref, u_ref, vt_ref, fg_ref, y_ref,
                       acc, hbuf0, hbuf1, act0, act1, *, eb, tn, n_blocks):
    jj = pl.program_id(1)
    hbuf, act = (hbuf0, hbuf1), (act0, act1)
    groups = eb // PEER_NKEYS
    assert groups == SUBLANES, "one aligned 8-row load of first-key scores per expert block"
    inv_sqrt2 = 1.0 / math.sqrt(2.0)

    tiles_per_it = math.gcd(PEER_LANE_TILES_PER_ITER, tn // LANES)
    n_it = tn // (tiles_per_it * LANES)
    h_rows = eb // n_it
    o_rows = D_MODEL // n_it

    def hidden(slot, rs):
        hbuf[slot][rs, :] = jnp.dot(u_ref[rs, :], xnt_ref[...], preferred_element_type=F32)

    def output(slot, rs):
        acc[rs, :] += jnp.dot(vt_ref[rs, :], act[slot][...], preferred_element_type=F32)

    def mixture(slot, ls):
        a0 = (jj - 1) * SUBLANES
        s1b = [st_ref[pl.ds(pl.multiple_of(h * 2 * PEER_NKEYS + a0, SUBLANES), SUBLANES), ls]
               for h in range(PEER_HEADS)]
        e1b = [e1_ref[pl.ds(pl.multiple_of(h * PEER_NKEYS + a0, SUBLANES), SUBLANES), ls]
               for h in range(PEER_HEADS)]
        taus = [tau_ref[h * SUBLANES:h * SUBLANES + 1, ls] for h in range(PEER_HEADS)]
        for b0 in range(0, PEER_NKEYS, PEER_KEY_STRIP):
            w = [jnp.zeros((PEER_KEY_STRIP, LANES), F32) for _ in range(groups)]
            for h in range(PEER_HEADS):
                r2 = h * 2 * PEER_NKEYS + PEER_NKEYS + b0
                s2 = st_ref[r2:r2 + PEER_KEY_STRIP, ls]
                e2 = e2_ref[h * PEER_NKEYS + b0:h * PEER_NKEYS + b0 + PEER_KEY_STRIP, ls]
                for ai in range(groups):
                    keep = s1b[h][ai:ai + 1, :] + s2 >= taus[h]
                    w[ai] = w[ai] + jnp.where(keep, e2 * e1b[h][ai:ai + 1, :], 0.0)
            for ai in range(groups):
                es = slice(ai * PEER_NKEYS + b0, ai * PEER_NKEYS + b0 + PEER_KEY_STRIP)
                h_t = hbuf[slot][es, ls]
                gelu = 0.5 * h_t * (1.0 + lax.erf(h_t * inv_sqrt2))
                act[slot][es, ls] = (w[ai] * gelu).astype(BF16)

    def sweep(slot, do_hidden, do_mixture, do_output):
        def step(it, carry):
            if do_output:
                output(1 - slot, pl.ds(pl.multiple_of(it * o_rows, o_rows), o_rows))
            if do_mixture:
                for k in range(tiles_per_it):
                    l0 = (it * tiles_per_it + k) * LANES
                    mixture(slot, pl.ds(pl.multiple_of(l0, LANES), LANES))
            if do_hidden:
                hidden(1 - slot, pl.ds(pl.multiple_of(it * h_rows, h_rows), h_rows))
            return carry
        lax.fori_loop(0, n_it, step, 0)

    @pl.when(jj == 0)
    def _():
        acc[...] = jnp.zeros(acc.shape, F32)
        act[0][...] = jnp.zeros(act[0].shape, BF16)
        sweep(0, True, False, False)

    for slot in range(2):
        @pl.when((jj >= 1) & (jj <= n_blocks) & (jj % 2 == slot))
        def _(slot=slot):
            sweep(slot, True, True, True)

    @pl.when(jj == n_blocks + 1)
    def _():
        sweep((n_blocks + 1) % 2, False, False, True)
        y = x1_ref[...] + acc[...].T
        y_ref[...] = _rms(y, fg_ref[...])


def _peer_dense(xnt, x1, st, e1, e2, tau, u_bf, vt_bf, final_g, tn, eb):
    n = x1.shape[0]
    n_blocks = PEER_N // eb
    tok = lambda w: pl.BlockSpec((tn, w), lambda i, j: (i, 0))
    col = lambda a: pl.BlockSpec((a.shape[0], tn), lambda i, j: (0, i))
    return pl.pallas_call(
        functools.partial(_peer_dense_kernel, eb=eb, tn=tn, n_blocks=n_blocks),
        grid=(n // tn, n_blocks + 2),
        in_specs=[col(xnt), tok(D_MODEL), col(st), col(e1), col(e2), col(tau),
                  pl.BlockSpec((eb, D_MODEL), lambda i, j: (jnp.minimum(j, n_blocks - 1), 0)),
                  pl.BlockSpec((D_MODEL, eb), lambda i, j: (0, jnp.clip(j - 2, 0, n_blocks - 1))),
                  pl.BlockSpec((1, D_MODEL), lambda i, j: (0, 0))],
        out_specs=tok(D_MODEL),
        out_shape=jax.ShapeDtypeStruct((n, D_MODEL), F32),
        scratch_shapes=[pltpu.VMEM((D_MODEL, tn), F32), pltpu.VMEM((eb, tn), F32), pltpu.VMEM((eb, tn), F32),
                        pltpu.VMEM((eb, tn), BF16), pltpu.VMEM((eb, tn), BF16)],
        compiler_params=_cparams(("parallel", "arbitrary")),
        name="peer_dense",
    )(xnt, x1, st, e1, e2, tau, u_bf, vt_bf, final_g.reshape(1, -1))


def _group(x, s0, moba_fn, wts, tm, tn, eb):
    (n1, w_in, wa2, ba, gla_g, w_pa, w_pb, w_out, n2, wq, keys, u_bf, vt_bf, final_g) = wts
    bsz, t, _ = x.shape
    n = bsz * t
    x2 = x.reshape(n, D_MODEL)
    qa, ka, va, ra, la, qb, kb, vb, ga, gb = _in_proj(x2, n1, w_in, wa2, ba, tm)
    r3 = lambda a: a.reshape(bsz, t, a.shape[-1])
    oa, s_fin = _gla(r3(qa), r3(ka), r3(va), r3(la), r3(ra), s0, gla_g)
    ob = moba_fn(r3(qb), r3(kb), r3(vb))
    x1, xn, st = _merge(x2, oa.reshape(n, GLA_V_W), ob.reshape(n, MOBA_W), ga, gb,
                        w_pa, w_pb, w_out, n2, wq, keys, tm)
    e1, e2, tau = _peer_select(st, tn)
    y = _peer_dense(xn, x1, st, e1, e2, tau, u_bf, vt_bf, final_g, tn, eb)
    kv_shape = (1, bsz, t, MOBA_HEADS, MOBA_DH)
    return y.reshape(bsz, t, D_MODEL), kb.reshape(kv_shape), vb.reshape(kv_shape), s_fin[None]


def kernel(x_prompt, x_sample, cache_k, cache_v, state_gla, page_table, norm1_g, w_in, gla_wa2, gla_ba,
           gla_norm_g, w_pa, w_pb, w_out, norm2_g, peer_wq, peer_keys, peer_u, peer_v, final_g):
    assert w_in.shape[0] == 1, "single-layer step"
    u_bf = peer_u[0].astype(BF16)
    vt_bf = peer_v[0].astype(BF16).T
    wts = (norm1_g[0], w_in[0], gla_wa2[0], gla_ba[0], gla_norm_g[0], w_pa[0], w_pb[0], w_out[0],
           norm2_g[0], peer_wq[0], peer_keys[0], u_bf, vt_bf, final_g)
    bp = x_prompt.shape[0]
    s0p = jnp.zeros((bp, GLA_HEADS, GLA_DK, GLA_DV), F32)
    n_s = x_sample.shape[0] * x_sample.shape[1]
    tok_blk = lambda n, pref: pref if n % pref == 0 else n
    n_p = bp * x_prompt.shape[1]
    yp, kp, vp, sp = _group(x_prompt, s0p, _moba_prompt, wts,
                            tok_blk(n_p, 256), tok_blk(n_p, 512), PEER_EXPERT_BLOCK)
    ck, cv = cache_k[0], cache_v[0]
    ys, ks, vs, ss = _group(x_sample, state_gla[0],
                            lambda q, k, v: _moba_sample(q, k, v, ck, cv, page_table), wts,
                            tok_blk(n_s, 256), tok_blk(n_s, 512), PEER_EXPERT_BLOCK)
    return (yp, ys, kp, vp, sp, ks, vs, ss)
```

```python
import functools
import math

import jax
import jax.numpy as jnp
from jax import lax
from jax.experimental import pallas as pl
from jax.experimental.pallas import tpu as pltpu

F32 = jnp.float32
BF16 = jnp.bfloat16

D_MODEL = 1024
GLA_HEADS = 4
GLA_DK = 64
GLA_DV = 128
GLA_RANK = 16
GLA_TAU = 16.0
GLA_CHUNK = 64
GLA_QK_W = GLA_HEADS * GLA_DK
GLA_V_W = GLA_HEADS * GLA_DV
MOBA_HEADS = 8
MOBA_DH = 64
MOBA_W = MOBA_HEADS * MOBA_DH
MOBA_BLOCK = 256
MOBA_TOPK = 3
MOBA_QCHUNK = 128
NEG = -1e30
PEER_HEADS = 8
PEER_NKEYS = 128
PEER_N = PEER_NKEYS * PEER_NKEYS
PEER_QDIM = 256
PEER_HALF = PEER_QDIM // 2
PEER_TOPK = 16
EPS = 1e-6
IN_SIZES = (GLA_QK_W, GLA_QK_W, GLA_V_W, GLA_V_W, GLA_RANK, MOBA_W, MOBA_W, MOBA_W, D_MODEL, D_MODEL)

LANES = 128
SUBLANES = 8
PEER_EXPERT_BLOCK = SUBLANES * PEER_NKEYS
PEER_KEY_STRIP = 32
PEER_LANE_TILES_PER_ITER = 2
VMEM_LIMIT = 56 * 1024 * 1024

_NT = (((1,), (1,)), ((), ()))
_TN = (((0,), (0,)), ((), ()))


def _cparams(sem):
    return pltpu.CompilerParams(dimension_semantics=sem, vmem_limit_bytes=VMEM_LIMIT)


def _rms(x, g):
    return x * lax.rsqrt(jnp.mean(x * x, axis=-1, keepdims=True) + EPS) * g


def _inproj_kernel(x_ref, g_ref, wqa, wka, wva, wra, wlr, wa2, ba, wqb, wkb, wvb, wga, wgb,
                   qa, ka, va, ra, la, qb, kb, vb, ga, gb):
    xn = _rms(x_ref[...], g_ref[...]).astype(BF16)
    for w, o in ((wqa, qa), (wka, ka), (wva, va), (wra, ra), (wqb, qb), (wkb, kb), (wvb, vb),
                 (wga, ga), (wgb, gb)):
        o[...] = jnp.dot(xn, w[...], preferred_element_type=F32).astype(o.dtype)
    lr = jnp.dot(xn, wlr[...], preferred_element_type=F32).astype(BF16)
    z = jnp.dot(lr, wa2[...], preferred_element_type=F32) + ba[...]
    la[...] = (jnp.minimum(z, 0.0) - jnp.log1p(jnp.exp(-jnp.abs(z)))) * (1.0 / GLA_TAU)


def _in_proj(x2, n1, w_in, wa2, ba, tm):
    n = x2.shape[0]
    offs = [0]
    for c in IN_SIZES:
        offs.append(offs[-1] + c)
    wb = w_in.astype(BF16)
    piece = lambda i: wb[:, offs[i]:offs[i + 1]]
    wlr = jnp.pad(piece(4), ((0, 0), (0, LANES - GLA_RANK)))
    wa2p = jnp.pad(wa2.astype(BF16), ((0, LANES - GLA_RANK), (0, 0)))
    weights = [piece(0), piece(1), piece(2), piece(3), wlr, wa2p, ba.reshape(1, -1),
               piece(5), piece(6), piece(7), piece(8), piece(9)]
    widths = [GLA_QK_W, GLA_QK_W, GLA_V_W, GLA_V_W, GLA_QK_W, MOBA_W, MOBA_W, MOBA_W, D_MODEL, D_MODEL]
    row = lambda w: pl.BlockSpec((tm, w), lambda i: (i, 0))
    full = lambda a: pl.BlockSpec(a.shape, lambda i: (0, 0))
    return pl.pallas_call(
        _inproj_kernel,
        grid=(n // tm,),
        in_specs=[row(D_MODEL), full(n1.reshape(1, -1))] + [full(w) for w in weights],
        out_specs=[row(w) for w in widths],
        out_shape=[jax.ShapeDtypeStruct((n, w), F32) for w in widths],
        compiler_params=_cparams(("parallel",)),
        name="in_proj",
    )(x2, n1.reshape(1, -1), *weights)


def _gla_kernel(qa, ka, va, la, ra, s0, g_ref, oa, sfin, s_scr, *, chunk, n_chunks):
    t = pl.program_id(1)

    @pl.when(t == 0)
    def _():
        s_scr[...] = s0[0]

    ri = lax.broadcasted_iota(jnp.int32, (chunk, chunk), 0)
    ci = lax.broadcasted_iota(jnp.int32, (chunk, chunk), 1)
    causal = ci <= ri
    tril = causal.astype(F32)
    eye_k = (lax.broadcasted_iota(jnp.int32, (GLA_DK, GLA_DK), 0)
             == lax.broadcasted_iota(jnp.int32, (GLA_DK, GLA_DK), 1))
    ones_kv = jnp.ones((GLA_DK, GLA_DV), F32)
    g = g_ref[...]

    def body(c, carry):
        r0 = pl.multiple_of(c * chunk, chunk)
        rows = pl.ds(r0, chunk)
        a = la[0, rows, :]
        b = jnp.dot(tril, a, preferred_element_type=F32, precision=lax.Precision.HIGHEST)
        b_last = b[chunk - 1:chunk, :]
        q = qa[0, rows, :] * (GLA_DK ** -0.5)
        k = ka[0, rows, :]
        q_dec = (q * jnp.exp(b)).astype(BF16)
        k_inv = (k * jnp.exp(-b)).astype(BF16)
        k_up = (k * jnp.exp(b_last - b)).astype(BF16)
        dec = jnp.exp(b_last)
        outs = []
        for h in range(GLA_HEADS):
            ks = slice(h * GLA_DK, (h + 1) * GLA_DK)
            vs = slice(h * GLA_DV, (h + 1) * GLA_DV)
            s = s_scr[h]
            v = va[0, rows, vs].astype(BF16)
            o = jnp.dot(q_dec[:, ks], s.astype(BF16), preferred_element_type=F32)
            att = lax.dot_general(q_dec[:, ks], k_inv[:, ks], _NT, preferred_element_type=F32)
            att = jnp.where(causal, att, 0.0).astype(BF16)
            o = o + jnp.dot(att, v, preferred_element_type=F32)
            kv = lax.dot_general(k_up[:, ks], v, _TN, preferred_element_type=F32)
            dmat = jnp.where(eye_k, jnp.broadcast_to(dec[:, ks], (GLA_DK, GLA_DK)), 0.0)
            dcol = jnp.dot(dmat, ones_kv, preferred_element_type=F32, precision=lax.Precision.HIGHEST)
            s_scr[h] = dcol * s + kv
            on = _rms(o, g)
            r = ra[0, rows, vs]
            outs.append(on * (r * jax.nn.sigmoid(r)))
        oa[0, rows, :] = jnp.concatenate(outs, axis=-1).astype(oa.dtype)
        return carry

    lax.fori_loop(0, n_chunks, body, 0)

    @pl.when(t == pl.num_programs(1) - 1)
    def _():
        sfin[0] = s_scr[...]


GLA_MIN_CHUNK = 16


def _gla(qa, ka, va, la, ra, s0, gla_g):
    bsz, t_real, _ = qa.shape
    chunk = math.gcd(GLA_CHUNK, t_real)
    if chunk < GLA_MIN_CHUNK:
        chunk = GLA_MIN_CHUNK
        pad = (-t_real) % chunk
        qa, ka, va, la, ra = (jnp.pad(a, ((0, 0), (0, pad), (0, 0))) for a in (qa, ka, va, la, ra))
    t = qa.shape[1]
    tb = min(t, 512)
    blk = lambda w: pl.BlockSpec((1, tb, w), lambda b, i: (b, i, 0))
    st = pl.BlockSpec((1, GLA_HEADS, GLA_DK, GLA_DV), lambda b, i: (b, 0, 0, 0))
    oa, s_fin = pl.pallas_call(
        functools.partial(_gla_kernel, chunk=chunk, n_chunks=tb // chunk),
        grid=(bsz, t // tb),
        in_specs=[blk(GLA_QK_W), blk(GLA_QK_W), blk(GLA_V_W), blk(GLA_QK_W), blk(GLA_V_W), st,
                  pl.BlockSpec((1, GLA_DV), lambda b, i: (0, 0))],
        out_specs=[blk(GLA_V_W), st],
        out_shape=[jax.ShapeDtypeStruct((bsz, t, GLA_V_W), BF16),
                   jax.ShapeDtypeStruct((bsz, GLA_HEADS, GLA_DK, GLA_DV), F32)],
        scratch_shapes=[pltpu.VMEM((GLA_HEADS, GLA_DK, GLA_DV), F32)],
        compiler_params=_cparams(("parallel", "arbitrary")),
        name="gla",
    )(qa, ka, va, la, ra, s0, gla_g.reshape(1, -1))
    return oa[:, :t_real], s_fin


def _alibi_slopes():
    return jnp.exp2(-8.0 * jnp.arange(1, MOBA_HEADS + 1, dtype=F32) / MOBA_HEADS)


def _topk_mask(gm, n_blocks):
    lane = lax.broadcasted_iota(jnp.int32, gm.shape, 1)
    rank = jnp.zeros(gm.shape, F32)
    for j in range(n_blocks):
        col = gm[:, j:j + 1]
        ahead = (col > gm) | ((col == gm) & (j < lane))
        rank = rank + jnp.where(ahead, 1.0, 0.0)
    return rank < float(MOBA_TOPK)


def _topk_mask_t(gm, n_blocks):
    row = lax.broadcasted_iota(jnp.int32, gm.shape, 0)
    rank = jnp.zeros(gm.shape, F32)
    for j in range(n_blocks):
        other = gm[j:j + 1, :]
        ahead = (other > gm) | ((other == gm) & (j < row))
        rank = rank + jnp.where(ahead, 1.0, 0.0)
    return rank < float(MOBA_TOPK)


def _moba_prompt_kernel(slopes_ref, q_ref, k_ref, v_ref, o_ref, *, seq):
    nb = seq // MOBA_BLOCK
    qrows = MOBA_BLOCK
    heads = LANES // MOBA_DH
    scale = MOBA_DH ** -0.5
    hp = pl.program_id(1)
    lane_q = lax.broadcasted_iota(jnp.int32, (qrows, LANES), 1)
    rel0 = (lax.broadcasted_iota(jnp.int32, (qrows, MOBA_BLOCK), 0)
            - lax.broadcasted_iota(jnp.int32, (qrows, MOBA_BLOCK), 1))
    causal = rel0 >= 0
    e_rows = lax.broadcasted_iota(jnp.int32, (LANES, MOBA_BLOCK), 0)
    nb_pad = -(-nb // SUBLANES) * SUBLANES
    assert nb_pad <= LANES
    blk_t = lax.broadcasted_iota(jnp.int32, (nb_pad, qrows), 0)
    hmask = [lane_q // MOBA_DH == x for x in range(heads)]
    slopes = [slopes_ref[hp * heads + x] for x in range(heads)]
    srel = [slopes[x] * rel0.astype(F32) for x in range(heads)]
    kmean = jnp.concatenate(
        [jnp.mean(k_ref[0, j * MOBA_BLOCK:(j + 1) * MOBA_BLOCK, :], axis=0, keepdims=True)
         for j in range(nb)] + [jnp.zeros((LANES - nb, LANES), F32)], axis=0)

    def q_block(cur, carry):
        r0 = pl.multiple_of(cur * qrows, qrows)
        q = q_ref[0, pl.ds(r0, qrows), :]
        past_t = blk_t < cur
        qx, sel = [], []
        for x in range(heads):
            qh = jnp.where(hmask[x], q, 0.0)
            gate_t = lax.dot_general(kmean, qh, _NT, preferred_element_type=F32,
                                     precision=lax.Precision.HIGHEST)[:nb_pad, :]
            keep_t = past_t & _topk_mask_t(jnp.where(past_t, gate_t, NEG), nb)
            sel_t = jnp.concatenate([jnp.where(keep_t, 1.0, 0.0), jnp.zeros((LANES - nb_pad, qrows), F32)], axis=0)
            sel.append(sel_t.T.astype(BF16))
            qx.append(qh.astype(BF16))

        kb = k_ref[0, pl.ds(r0, qrows), :].astype(BF16)
        vb = v_ref[0, pl.ds(r0, qrows), :].astype(BF16)
        init = []
        for x in range(heads):
            s = lax.dot_general(qx[x], kb, _NT, preferred_element_type=F32) * scale - srel[x]
            s = jnp.where(causal, s, NEG)
            m = jnp.max(s, axis=-1, keepdims=True)
            p = jnp.exp(s - m)
            init.append((m, jnp.sum(p, axis=-1, keepdims=True),
                         jnp.dot(p.astype(BF16), vb, preferred_element_type=F32)))

        def past_block(j, st):
            k0 = pl.multiple_of(j * MOBA_BLOCK, MOBA_BLOCK)
            kb = k_ref[0, pl.ds(k0, MOBA_BLOCK), :].astype(BF16)
            vb = v_ref[0, pl.ds(k0, MOBA_BLOCK), :].astype(BF16)
            ej = jnp.where(e_rows == j, 1.0, 0.0).astype(BF16)
            off = (r0 - k0).astype(F32)
            out = []
            for x in range(heads):
                m, l, acc = st[x]
                selb = jnp.dot(sel[x], ej, preferred_element_type=F32)
                s = lax.dot_general(qx[x], kb, _NT, preferred_element_type=F32) * scale
                s = s - (srel[x] + slopes[x] * off)
                s = jnp.where(selb > 0.5, s, NEG)
                m_new = jnp.maximum(m, jnp.max(s, axis=-1, keepdims=True))
                alpha = jnp.exp(m - m_new)
                p = jnp.exp(s - m_new)
                acc = alpha * acc + jnp.dot(p.astype(BF16), vb, preferred_element_type=F32)
                out.append((m_new, alpha * l + jnp.sum(p, axis=-1, keepdims=True), acc))
            return tuple(out)

        st = lax.fori_loop(0, cur, past_block, tuple(init))
        o = st[heads - 1][2] / st[heads - 1][1]
        for x in range(heads - 2, -1, -1):
            o = jnp.where(hmask[x], st[x][2] / st[x][1], o)
        o_ref[0, pl.ds(r0, qrows), :] = o.astype(o_ref.dtype)
        return carry

    lax.fori_loop(0, nb, q_block, 0)


def _moba_prompt(qb, kb, vb):
    bsz, t, _ = qb.shape
    blk = pl.BlockSpec((1, t, LANES), lambda b, h: (b, 0, h))
    return pl.pallas_call(
        functools.partial(_moba_prompt_kernel, seq=t),
        grid=(bsz, MOBA_W // LANES),
        in_specs=[pl.BlockSpec(memory_space=pltpu.SMEM), blk, blk, blk],
        out_specs=blk,
        out_shape=jax.ShapeDtypeStruct((bsz, t, MOBA_W), BF16),
        compiler_params=_cparams(("parallel", "parallel")),
        name="moba_prompt",
    )(_alibi_slopes(), qb, kb, vb)


def _moba_sample_kernel(pt_ref, slopes_ref, q_ref, kn_ref, vn_ref, *rest, n_new, n_past_blocks, past_len,
                        blocks_per_step, pages_per_block):
    del pt_ref
    n_pg = blocks_per_step * pages_per_block
    k_refs, v_refs = rest[:n_pg], rest[n_pg:2 * n_pg]
    o_ref, qbd, gate_s, m_s, l_s, o_s = rest[2 * n_pg:]
    j = pl.program_id(1)
    rows = n_new * MOBA_HEADS
    page = k_refs[0].shape[2]
    scale = MOBA_DH ** -0.5
    row_i = lax.broadcasted_iota(jnp.int32, (rows, 1), 0)
    head_r = row_i % MOBA_HEADS
    tok_r = row_i // MOBA_HEADS
    slope = jnp.zeros((rows, 1), F32)
    for h in range(MOBA_HEADS):
        slope = jnp.where(head_r == h, slopes_ref[h], slope)
    head_mask = (lax.broadcasted_iota(jnp.int32, (rows, MOBA_W), 1) // MOBA_DH
                 == lax.broadcasted_iota(jnp.int32, (rows, MOBA_W), 0) % MOBA_HEADS)
    key_i = lax.broadcasted_iota(jnp.int32, (rows, page), 1)
    lane_nb = lax.broadcasted_iota(jnp.int32, (rows, LANES), 1)

    @pl.when(j == 0)
    def _():
        q = q_ref[0]
        qrep = jnp.concatenate([jnp.broadcast_to(q[t:t + 1, :], (MOBA_HEADS, MOBA_W))
                                for t in range(n_new)], axis=0)
        qbd[...] = jnp.where(head_mask, qrep, 0.0)
        gate_s[...] = jnp.full(gate_s.shape, NEG, F32)
        m_s[...] = jnp.full(m_s.shape, NEG, F32)
        l_s[...] = jnp.zeros(l_s.shape, F32)

    qf = qbd[...]
    q16 = qf.astype(BF16)
    for bi in range(blocks_per_step):
        blk = j * blocks_per_step + bi
        ss = []
        gate_j = jnp.zeros((rows, 1), F32)
        for o in range(pages_per_block):
            kt = k_refs[bi * pages_per_block + o][0].astype(BF16)
            raw = jnp.dot(q16, kt, preferred_element_type=F32)
            gate_j = gate_j + jnp.sum(raw, axis=-1, keepdims=True)
            dist = (past_len - blk * MOBA_BLOCK - o * page) + tok_r - key_i
            ss.append(raw * scale - slope * dist.astype(F32))
        gate_j = gate_j * (1.0 / MOBA_BLOCK)
        m = jnp.max(ss[0], axis=-1, keepdims=True)
        for s in ss[1:]:
            m = jnp.maximum(m, jnp.max(s, axis=-1, keepdims=True))
        l = jnp.zeros((rows, 1), F32)
        o_blk = jnp.zeros((rows, MOBA_W), F32)
        for o, s in enumerate(ss):
            p = jnp.exp(s - m)
            l = l + jnp.sum(p, axis=-1, keepdims=True)
            vt = v_refs[bi * pages_per_block + o][0].astype(BF16)
            o_blk = o_blk + lax.dot_general(p.astype(BF16), vt, _NT, preferred_element_type=F32)
        here = lane_nb == blk
        gate_s[...] = jnp.where(here, gate_j, gate_s[...])
        m_s[...] = jnp.where(here, m, m_s[...])
        l_s[...] = jnp.where(here, l, l_s[...])
        o_s[blk] = o_blk

    @pl.when(j == pl.num_programs(1) - 1)
    def _():
        sel = _topk_mask(gate_s[...], n_past_blocks) & (lane_nb < n_past_blocks)
        kn = kn_ref[0]
        vn = vn_ref[0]
        s_own = []
        for c in range(n_new):
            sc = jnp.sum(qf * kn[c:c + 1, :], axis=-1, keepdims=True) * scale
            sc = sc - slope * (tok_r - c).astype(F32)
            s_own.append(jnp.where(tok_r >= c, sc, NEG))
        m_all = jnp.where(sel, m_s[...], NEG)
        m_tot = jnp.max(m_all, axis=-1, keepdims=True)
        for sc in s_own:
            m_tot = jnp.maximum(m_tot, sc)
        w = jnp.where(sel, jnp.exp(m_all - m_tot), 0.0)
        denom = jnp.sum(w * l_s[...], axis=-1, keepdims=True)
        num = jnp.zeros((rows, MOBA_W), F32)
        for c, sc in enumerate(s_own):
            pc = jnp.exp(sc - m_tot)
            denom = denom + pc
            num = num + pc * vn[c:c + 1, :]

        def add_block(jj, acc):
            wj = jnp.sum(jnp.where(lane_nb == jj, w, 0.0), axis=-1, keepdims=True)
            return acc + wj * o_s[jj]

        num = lax.fori_loop(0, n_past_blocks, add_block, num)
        res = jnp.where(head_mask, num / denom, 0.0)
        o_ref[0] = jnp.concatenate(
            [jnp.sum(res[t * MOBA_HEADS:(t + 1) * MOBA_HEADS, :], axis=0, keepdims=True)
             for t in range(n_new)], axis=0).astype(o_ref.dtype)


MOBA_SAMPLE_BLOCKS_PER_STEP = 2


def _moba_sample(qb, kb, vb, cache_k, cache_v, page_table):
    bsz, n_new, _ = qb.shape
    n_pages = page_table.shape[1]
    page = cache_k.shape[1]
    ppb = MOBA_BLOCK // page
    assert ppb * page == MOBA_BLOCK and n_pages % ppb == 0
    nbp = n_pages // ppb
    bps = MOBA_SAMPLE_BLOCKS_PER_STEP if nbp % MOBA_SAMPLE_BLOCKS_PER_STEP == 0 else 1
    assert nbp <= LANES
    rows = n_new * MOBA_HEADS
    to_pages = lambda c: jnp.transpose(c, (0, 2, 3, 1)).reshape(c.shape[0], MOBA_W, page)
    new = pl.BlockSpec((1, n_new, MOBA_W), lambda b, j, pt: (b, 0, 0))
    n_pg = bps * ppb
    pg = lambda o: pl.BlockSpec((1, MOBA_W, page), lambda b, j, pt: (pt[b, n_pg * j + o], 0, 0))
    grid_spec = pltpu.PrefetchScalarGridSpec(
        num_scalar_prefetch=1,
        grid=(bsz, nbp // bps),
        in_specs=[pl.BlockSpec(memory_space=pltpu.SMEM), new, new, new]
                 + [pg(o) for o in range(n_pg)] + [pg(o) for o in range(n_pg)],
        out_specs=new,
        scratch_shapes=[pltpu.VMEM((rows, MOBA_W), F32), pltpu.VMEM((rows, LANES), F32),
                        pltpu.VMEM((rows, LANES), F32), pltpu.VMEM((rows, LANES), F32),
                        pltpu.VMEM((nbp, rows, MOBA_W), F32)],
    )
    ck, cv = to_pages(cache_k), to_pages(cache_v)
    return pl.pallas_call(
        functools.partial(_moba_sample_kernel, n_new=n_new, n_past_blocks=nbp, past_len=n_pages * page,
                          blocks_per_step=bps, pages_per_block=ppb),
        grid_spec=grid_spec,
        out_shape=jax.ShapeDtypeStruct((bsz, n_new, MOBA_W), BF16),
        compiler_params=_cparams(("parallel", "arbitrary")),
        name="moba_sample",
    )(page_table, _alibi_slopes(), qb, kb, vb, *([ck] * n_pg), *([cv] * n_pg))


def _merge_kernel(x_ref, oa_ref, ob_ref, ga_ref, gb_ref, wpa, wpb, wout, n2, wq, keys,
                  x1_ref, xnt_ref, st_ref):
    ya = jnp.dot(oa_ref[...], wpa[...], preferred_element_type=F32)
    yb = jnp.dot(ob_ref[...], wpb[...], preferred_element_type=F32)
    merged = jax.nn.sigmoid(ga_ref[...]) * ya + jax.nn.sigmoid(gb_ref[...]) * yb
    x1 = x_ref[...] + jnp.dot(merged.astype(BF16), wout[...], preferred_element_type=F32)
    x1_ref[...] = x1
    xn = _rms(x1, n2[...])
    xnt_ref[...] = xn.T.astype(BF16)
    q = jnp.dot(xn.astype(BF16), wq[...], preferred_element_type=F32).astype(BF16)
    for i in range(2 * PEER_HEADS):
        st_ref[i * PEER_NKEYS:(i + 1) * PEER_NKEYS, :] = lax.dot_general(
            keys[i], q[:, i * PEER_HALF:(i + 1) * PEER_HALF], _NT, preferred_element_type=F32)


def _merge(x2, oa, ob, ga, gb, w_pa, w_pb, w_out, n2, wq, keys, tm):
    n = x2.shape[0]
    ws = [w_pa.astype(BF16), w_pb.astype(BF16), w_out.astype(BF16), n2.reshape(1, -1), wq.astype(BF16),
          keys.astype(BF16).reshape(2 * PEER_HEADS, PEER_NKEYS, PEER_HALF)]
    row = lambda w: pl.BlockSpec((tm, w), lambda i: (i, 0))
    full = lambda a: pl.BlockSpec(a.shape, lambda i: (0,) * a.ndim)
    n_s = 2 * PEER_HEADS * PEER_NKEYS
    return pl.pallas_call(
        _merge_kernel,
        grid=(n // tm,),
        in_specs=[row(D_MODEL), row(GLA_V_W), row(MOBA_W), row(D_MODEL), row(D_MODEL)] + [full(w) for w in ws],
        out_specs=[row(D_MODEL), pl.BlockSpec((D_MODEL, tm), lambda i: (0, i)),
                   pl.BlockSpec((n_s, tm), lambda i: (0, i))],
        out_shape=[jax.ShapeDtypeStruct((n, D_MODEL), F32), jax.ShapeDtypeStruct((D_MODEL, n), BF16),
                   jax.ShapeDtypeStruct((n_s, n), F32)],
        compiler_params=_cparams(("parallel",)),
        name="merge",
    )(x2, oa, ob, ga, gb, *ws)


def _extract_top(vals, n):
    out = []
    for i in range(n):
        m = jnp.max(vals, axis=0, keepdims=True)
        out.append(m)
        if i + 1 < n:
            vals = jnp.where(vals == m, -jnp.inf, vals)
    return out


def _peer_select_kernel(st_ref, e1_ref, e2_ref, tau_ref, *, lane_tiles):
    def head(idx, carry):
        h = idx // lane_tiles
        ls = pl.ds(pl.multiple_of((idx % lane_tiles) * LANES, LANES), LANES)
        r1 = pl.multiple_of(h * 2 * PEER_NKEYS, 2 * PEER_NKEYS)
        r2 = pl.multiple_of(h * 2 * PEER_NKEYS + PEER_NKEYS, PEER_NKEYS)
        ro = pl.multiple_of(h * PEER_NKEYS, PEER_NKEYS)
        s1 = st_ref[pl.ds(r1, PEER_NKEYS), ls]
        s2 = st_ref[pl.ds(r2, PEER_NKEYS), ls]
        v1 = _extract_top(s1, PEER_TOPK)
        v2 = _extract_top(s2, PEER_TOPK)
        v1a = jnp.concatenate(v1, axis=0)
        v2a = jnp.concatenate(v2, axis=0)
        cands = [v1[0] + v2a, v1[1] + v2a[:8], v1[2] + v2a[:8], v1[3] + v2a[:8]]
        cands += [v1[i] + v2a[:8] for i in range(4, 8)]
        cands.append(v1a[8:] + v2[0])
        cand = jnp.concatenate(cands, axis=0)
        best = _extract_top(cand, PEER_TOPK)
        z = jnp.ones_like(best[0])
        for bk in best[1:]:
            z = z + jnp.exp(bk - best[0])
        e1_ref[pl.ds(ro, PEER_NKEYS), ls] = jnp.exp(s1 - v1[0]) / z
        e2_ref[pl.ds(ro, PEER_NKEYS), ls] = jnp.exp(s2 - v2[0])
        tau_ref[pl.ds(pl.multiple_of(h * SUBLANES, SUBLANES), SUBLANES), ls] = jnp.broadcast_to(
            best[-1], (SUBLANES, LANES))
        return carry

    lax.fori_loop(0, PEER_HEADS * lane_tiles, head, 0)


def _peer_select(st, tn):
    n_s, n = st.shape
    n_e = PEER_HEADS * PEER_NKEYS
    return pl.pallas_call(
        functools.partial(_peer_select_kernel, lane_tiles=tn // LANES),
        grid=(n // tn,),
        in_specs=[pl.BlockSpec((n_s, tn), lambda i: (0, i))],
        out_specs=[pl.BlockSpec((n_e, tn), lambda i: (0, i)), pl.BlockSpec((n_e, tn), lambda i: (0, i)),
                   pl.BlockSpec((PEER_HEADS * SUBLANES, tn), lambda i: (0, i))],
        out_shape=[jax.ShapeDtypeStruct((n_e, n), F32), jax.ShapeDtypeStruct((n_e, n), F32),
                   jax.ShapeDtypeStruct((PEER_HEADS * SUBLANES, n), F32)],
        compiler_params=_cparams(("parallel",)),
        name="peer_select",
    )(st)


def _peer_dense_kernel(xnt_ref, x1_ref, st_ref, e1_ref, e2_ref, tau_ref, u_ref, vt_ref, fg_ref, y_ref,
                       acc, hbuf0, hbuf1, act0, act1, *, eb, tn, n_blocks):
    jj = pl.program_id(1)
    hbuf, act = (hbuf0, hbuf1), (act0, act1)
    groups = eb // PEER_NKEYS
    assert groups == SUBLANES, "one aligned 8-row load of first-key scores per expert block"
    inv_sqrt2 = 1.0 / math.sqrt(2.0)

    tiles_per_it = math.gcd(PEER_LANE_TILES_PER_ITER, tn // LANES)
    n_it = tn // (tiles_per_it * LANES)
    h_rows = eb // n_it
    o_rows = D_MODEL // n_it

    def hidden(slot, rs):
        hbuf[slot][rs, :] = jnp.dot(u_ref[rs, :], xnt_ref[...], preferred_element_type=F32)

    def output(slot, rs):
        acc[rs, :] += jnp.dot(vt_ref[rs, :], act[slot][...], preferred_element_type=F32)

    def mixture(slot, ls):
        a0 = (jj - 1) * SUBLANES
        s1b = [st_ref[pl.ds(pl.multiple_of(h * 2 * PEER_NKEYS + a0, SUBLANES), SUBLANES), ls]
               for h in range(PEER_HEADS)]
        e1b = [e1_ref[pl.ds(pl.multiple_of(h * PEER_NKEYS + a0, SUBLANES), SUBLANES), ls]
               for h in range(PEER_HEADS)]
        taus = [tau_ref[h * SUBLANES:h * SUBLANES + 1, ls] for h in range(PEER_HEADS)]
        for b0 in range(0, PEER_NKEYS, PEER_KEY_STRIP):
            w = [jnp.zeros((PEER_KEY_STRIP, LANES), F32) for _ in range(groups)]
            for h in range(PEER_HEADS):
                r2 = h * 2 * PEER_NKEYS + PEER_NKEYS + b0
                s2 = st_ref[r2:r2 + PEER_KEY_STRIP, ls]
                e2 = e2_ref[h * PEER_NKEYS + b0:h * PEER_NKEYS + b0 + PEER_KEY_STRIP, ls]
                for ai in range(groups):
                    keep = s1b[h][ai:ai + 1, :] + s2 >= taus[h]
                    w[ai] = w[ai] + jnp.where(keep, e2 * e1b[h][ai:ai + 1, :], 0.0)
            for ai in range(groups):
                es = slice(ai * PEER_NKEYS + b0, ai * PEER_NKEYS + b0 + PEER_KEY_STRIP)
                h_t = hbuf[slot][es, ls]
                gelu = 0.5 * h_t * (1.0 + lax.erf(h_t * inv_sqrt2))
                act[slot][es, ls] = (w[ai] * gelu).astype(BF16)

    def sweep(slot, do_hidden, do_mixture, do_output):
        def step(it, carry):
            if do_output:
                output(1 - slot, pl.ds(pl.multiple_of(it * o_rows, o_rows), o_rows))
            if do_mixture:
                for k in range(tiles_per_it):
                    l0 = (it * tiles_per_it + k) * LANES
                    mixture(slot, pl.ds(pl.multiple_of(l0, LANES), LANES))
            if do_hidden:
                hidden(1 - slot, pl.ds(pl.multiple_of(it * h_rows, h_rows), h_rows))
            return carry
        lax.fori_loop(0, n_it, step, 0)

    @pl.when(jj == 0)
    def _():
        acc[...] = jnp.zeros(acc.shape, F32)
        act[0][...] = jnp.zeros(act[0].shape, BF16)
        sweep(0, True, False, False)

    for slot in range(2):
        @pl.when((jj >= 1) & (jj <= n_blocks) & (jj % 2 == slot))
        def _(slot=slot):
            sweep(slot, True, True, True)

    @pl.when(jj == n_blocks + 1)
    def _():
        sweep((n_blocks + 1) % 2, False, False, True)
        y = x1_ref[...] + acc[...].T
        y_ref[...] = _rms(y, fg_ref[...])


def _peer_dense(xnt, x1, st, e1, e2, tau, u_bf, vt_bf, final_g, tn, eb):
    n = x1.shape[0]
    n_blocks = PEER_N // eb
    tok = lambda w: pl.BlockSpec((tn, w), lambda i, j: (i, 0))
    col = lambda a: pl.BlockSpec((a.shape[0], tn), lambda i, j: (0, i))
    return pl.pallas_call(
        functools.partial(_peer_dense_kernel, eb=eb, tn=tn, n_blocks=n_blocks),
        grid=(n // tn, n_blocks + 2),
        in_specs=[col(xnt), tok(D_MODEL), col(st), col(e1), col(e2), col(tau),
                  pl.BlockSpec((eb, D_MODEL), lambda i, j: (jnp.minimum(j, n_blocks - 1), 0)),
                  pl.BlockSpec((D_MODEL, eb), lambda i, j: (0, jnp.clip(j - 2, 0, n_blocks - 1))),
                  pl.BlockSpec((1, D_MODEL), lambda i, j: (0, 0))],
        out_specs=tok(D_MODEL),
        out_shape=jax.ShapeDtypeStruct((n, D_MODEL), F32),
        scratch_shapes=[pltpu.VMEM((D_MODEL, tn), F32), pltpu.VMEM((eb, tn), F32), pltpu.VMEM((eb, tn), F32),
                        pltpu.VMEM((eb, tn), BF16), pltpu.VMEM((eb, tn), BF16)],
        compiler_params=_cparams(("parallel", "arbitrary")),
        name="peer_dense",
    )(xnt, x1, st, e1, e2, tau, u_bf, vt_bf, final_g.reshape(1, -1))


def _group(x, s0, moba_fn, wts, tm, tn, eb):
    (n1, w_in, wa2, ba, gla_g, w_pa, w_pb, w_out, n2, wq, keys, u_bf, vt_bf, final_g) = wts
    bsz, t, _ = x.shape
    n = bsz * t
    x2 = x.reshape(n, D_MODEL)
    qa, ka, va, ra, la, qb, kb, vb, ga, gb = _in_proj(x2, n1, w_in, wa2, ba, tm)
    r3 = lambda a: a.reshape(bsz, t, a.shape[-1])
    oa, s_fin = _gla(r3(qa), r3(ka), r3(va), r3(la), r3(ra), s0, gla_g)
    ob = moba_fn(r3(qb), r3(kb), r3(vb))
    x1, xn, st = _merge(x2, oa.reshape(n, GLA_V_W), ob.reshape(n, MOBA_W), ga, gb,
                        w_pa, w_pb, w_out, n2, wq, keys, tm)
    e1, e2, tau = _peer_select(st, tn)
    y = _peer_dense(xn, x1, st, e1, e2, tau, u_bf, vt_bf, final_g, tn, eb)
    kv_shape = (1, bsz, t, MOBA_HEADS, MOBA_DH)
    return y.reshape(bsz, t, D_MODEL), kb.reshape(kv_shape), vb.reshape(kv_shape), s_fin[None]


def kernel(x_prompt, x_sample, cache_k, cache_v, state_gla, page_table, norm1_g, w_in, gla_wa2, gla_ba,
           gla_norm_g, w_pa, w_pb, w_out, norm2_g, peer_wq, peer_keys, peer_u, peer_v, final_g):
    assert w_in.shape[0] == 1, "single-layer step"
    u_bf = peer_u[0].astype(BF16)
    vt_bf = peer_v[0].astype(BF16).T
    wts = (norm1_g[0], w_in[0], gla_wa2[0], gla_ba[0], gla_norm_g[0], w_pa[0], w_pb[0], w_out[0],
           norm2_g[0], peer_wq[0], peer_keys[0], u_bf, vt_bf, final_g)
    bp = x_prompt.shape[0]
    s0p = jnp.zeros((bp, GLA_HEADS, GLA_DK, GLA_DV), F32)
    n_s = x_sample.shape[0] * x_sample.shape[1]
    tok_blk = lambda n, pref: pref if n % pref == 0 else n
    n_p = bp * x_prompt.shape[1]
    yp, kp, vp, sp = _group(x_prompt, s0p, _moba_prompt, wts,
                            tok_blk(n_p, 256), tok_blk(n_p, 512), PEER_EXPERT_BLOCK)
    ck, cv = cache_k[0], cache_v[0]
    ys, ks, vs, ss = _group(x_sample, state_gla[0],
                            lambda q, k, v: _moba_sample(q, k, v, ck, cv, page_table), wts,
                            tok_blk(n_s, 256), tok_blk(n_s, 512), PEER_EXPERT_BLOCK)
    return (yp, ys, kp, vp, sp, ks, vs, ss)
```

```python
import functools
import math

import jax
import jax.numpy as jnp
from jax import lax
from jax.experimental import pallas as pl
from jax.experimental.pallas import tpu as pltpu

F32 = jnp.float32
BF16 = jnp.bfloat16

D_MODEL = 1024
GLA_HEADS = 4
GLA_DK = 64
GLA_DV = 128
GLA_RANK = 16
GLA_TAU = 16.0
GLA_CHUNK = 64
GLA_QK_W = GLA_HEADS * GLA_DK
GLA_V_W = GLA_HEADS * GLA_DV
MOBA_HEADS = 8
MOBA_DH = 64
MOBA_W = MOBA_HEADS * MOBA_DH
MOBA_BLOCK = 256
MOBA_TOPK = 3
MOBA_QCHUNK = 128
NEG = -1e30
PEER_HEADS = 8
PEER_NKEYS = 128
PEER_N = PEER_NKEYS * PEER_NKEYS
PEER_QDIM = 256
PEER_HALF = PEER_QDIM // 2
PEER_TOPK = 16
EPS = 1e-6
IN_SIZES = (GLA_QK_W, GLA_QK_W, GLA_V_W, GLA_V_W, GLA_RANK, MOBA_W, MOBA_W, MOBA_W, D_MODEL, D_MODEL)

LANES = 128
SUBLANES = 8
PEER_EXPERT_BLOCK = SUBLANES * PEER_NKEYS
PEER_KEY_STRIP = 16
PEER_LANE_TILES_PER_ITER = 2
VMEM_LIMIT = 56 * 1024 * 1024

_NT = (((1,), (1,)), ((), ()))
_TN = (((0,), (0,)), ((), ()))


def _cparams(sem):
    return pltpu.CompilerParams(dimension_semantics=sem, vmem_limit_bytes=VMEM_LIMIT)


def _rms(x, g):
    return x * lax.rsqrt(jnp.mean(x * x, axis=-1, keepdims=True) + EPS) * g


def _inproj_kernel(x_ref, g_ref, wqa, wka, wva, wra, wlr, wa2, ba, wqb, wkb, wvb, wga, wgb,
                   qa, ka, va, ra, la, qb, kb, vb, ga, gb):
    xn = _rms(x_ref[...], g_ref[...]).astype(BF16)
    for w, o in ((wqa, qa), (wka, ka), (wva, va), (wra, ra), (wqb, qb), (wkb, kb), (wvb, vb),
                 (wga, ga), (wgb, gb)):
        o[...] = jnp.dot(xn, w[...], preferred_element_type=F32).astype(o.dtype)
    lr = jnp.dot(xn, wlr[...], preferred_element_type=F32).astype(BF16)
    z = jnp.dot(lr, wa2[...], preferred_element_type=F32) + ba[...]
    la[...] = (jnp.minimum(z, 0.0) - jnp.log1p(jnp.exp(-jnp.abs(z)))) * (1.0 / GLA_TAU)


def _in_proj(x2, n1, w_in, wa2, ba, tm):
    n = x2.shape[0]
    offs = [0]
    for c in IN_SIZES:
        offs.append(offs[-1] + c)
    wb = w_in.astype(BF16)
    piece = lambda i: wb[:, offs[i]:offs[i + 1]]
    wlr = jnp.pad(piece(4), ((0, 0), (0, LANES - GLA_RANK)))
    wa2p = jnp.pad(wa2.astype(BF16), ((0, LANES - GLA_RANK), (0, 0)))
    weights = [piece(0), piece(1), piece(2), piece(3), wlr, wa2p, ba.reshape(1, -1),
               piece(5), piece(6), piece(7), piece(8), piece(9)]
    widths = [GLA_QK_W, GLA_QK_W, GLA_V_W, GLA_V_W, GLA_QK_W, MOBA_W, MOBA_W, MOBA_W, D_MODEL, D_MODEL]
    row = lambda w: pl.BlockSpec((tm, w), lambda i: (i, 0))
    full = lambda a: pl.BlockSpec(a.shape, lambda i: (0, 0))
    return pl.pallas_call(
        _inproj_kernel,
        grid=(n // tm,),
        in_specs=[row(D_MODEL), full(n1.reshape(1, -1))] + [full(w) for w in weights],
        out_specs=[row(w) for w in widths],
        out_shape=[jax.ShapeDtypeStruct((n, w), F32) for w in widths],
        compiler_params=_cparams(("parallel",)),
        name="in_proj",
    )(x2, n1.reshape(1, -1), *weights)


def _gla_kernel(qa, ka, va, la, ra, s0, g_ref, oa, sfin, s_scr, *, chunk, n_chunks):
    t = pl.program_id(1)

    @pl.when(t == 0)
    def _():
        s_scr[...] = s0[0]

    ri = lax.broadcasted_iota(jnp.int32, (chunk, chunk), 0)
    ci = lax.broadcasted_iota(jnp.int32, (chunk, chunk), 1)
    causal = ci <= ri
    tril = causal.astype(F32)
    eye_k = (lax.broadcasted_iota(jnp.int32, (GLA_DK, GLA_DK), 0)
             == lax.broadcasted_iota(jnp.int32, (GLA_DK, GLA_DK), 1))
    ones_kv = jnp.ones((GLA_DK, GLA_DV), F32)
    g = g_ref[...]

    def body(c, carry):
        r0 = pl.multiple_of(c * chunk, chunk)
        rows = pl.ds(r0, chunk)
        a = la[0, rows, :]
        b = jnp.dot(tril, a, preferred_element_type=F32, precision=lax.Precision.HIGHEST)
        b_last = b[chunk - 1:chunk, :]
        q = qa[0, rows, :] * (GLA_DK ** -0.5)
        k = ka[0, rows, :]
        q_dec = (q * jnp.exp(b)).astype(BF16)
        k_inv = (k * jnp.exp(-b)).astype(BF16)
        k_up = (k * jnp.exp(b_last - b)).astype(BF16)
        dec = jnp.exp(b_last)
        outs = []
        for h in range(GLA_HEADS):
            ks = slice(h * GLA_DK, (h + 1) * GLA_DK)
            vs = slice(h * GLA_DV, (h + 1) * GLA_DV)
            s = s_scr[h]
            v = va[0, rows, vs].astype(BF16)
            o = jnp.dot(q_dec[:, ks], s.astype(BF16), preferred_element_type=F32)
            att = lax.dot_general(q_dec[:, ks], k_inv[:, ks], _NT, preferred_element_type=F32)
            att = jnp.where(causal, att, 0.0).astype(BF16)
            o = o + jnp.dot(att, v, preferred_element_type=F32)
            kv = lax.dot_general(k_up[:, ks], v, _TN, preferred_element_type=F32)
            dmat = jnp.where(eye_k, jnp.broadcast_to(dec[:, ks], (GLA_DK, GLA_DK)), 0.0)
            dcol = jnp.dot(dmat, ones_kv, preferred_element_type=F32, precision=lax.Precision.HIGHEST)
            s_scr[h] = dcol * s + kv
            on = _rms(o, g)
            r = ra[0, rows, vs]
            outs.append(on * (r * jax.nn.sigmoid(r)))
        oa[0, rows, :] = jnp.concatenate(outs, axis=-1).astype(oa.dtype)
        return carry

    lax.fori_loop(0, n_chunks, body, 0)

    @pl.when(t == pl.num_programs(1) - 1)
    def _():
        sfin[0] = s_scr[...]


GLA_MIN_CHUNK = 16


def _gla(qa, ka, va, la, ra, s0, gla_g):
    bsz, t_real, _ = qa.shape
    chunk = math.gcd(GLA_CHUNK, t_real)
    if chunk < GLA_MIN_CHUNK:
        chunk = GLA_MIN_CHUNK
        pad = (-t_real) % chunk
        qa, ka, va, la, ra = (jnp.pad(a, ((0, 0), (0, pad), (0, 0))) for a in (qa, ka, va, la, ra))
    t = qa.shape[1]
    tb = min(t, 512)
    blk = lambda w: pl.BlockSpec((1, tb, w), lambda b, i: (b, i, 0))
    st = pl.BlockSpec((1, GLA_HEADS, GLA_DK, GLA_DV), lambda b, i: (b, 0, 0, 0))
    oa, s_fin = pl.pallas_call(
        functools.partial(_gla_kernel, chunk=chunk, n_chunks=tb // chunk),
        grid=(bsz, t // tb),
        in_specs=[blk(GLA_QK_W), blk(GLA_QK_W), blk(GLA_V_W), blk(GLA_QK_W), blk(GLA_V_W), st,
                  pl.BlockSpec((1, GLA_DV), lambda b, i: (0, 0))],
        out_specs=[blk(GLA_V_W), st],
        out_shape=[jax.ShapeDtypeStruct((bsz, t, GLA_V_W), BF16),
                   jax.ShapeDtypeStruct((bsz, GLA_HEADS, GLA_DK, GLA_DV), F32)],
        scratch_shapes=[pltpu.VMEM((GLA_HEADS, GLA_DK, GLA_DV), F32)],
        compiler_params=_cparams(("parallel", "arbitrary")),
        name="gla",
    )(qa, ka, va, la, ra, s0, gla_g.reshape(1, -1))
    return oa[:, :t_real], s_fin


def _alibi_slopes():
    return jnp.exp2(-8.0 * jnp.arange(1, MOBA_HEADS + 1, dtype=F32) / MOBA_HEADS)


def _topk_mask(gm, n_blocks):
    lane = lax.broadcasted_iota(jnp.int32, gm.shape, 1)
    rank = jnp.zeros(gm.shape, F32)
    for j in range(n_blocks):
        col = gm[:, j:j + 1]
        ahead = (col > gm) | ((col == gm) & (j < lane))
        rank = rank + jnp.where(ahead, 1.0, 0.0)
    return rank < float(MOBA_TOPK)


def _topk_mask_t(gm, n_blocks):
    row = lax.broadcasted_iota(jnp.int32, gm.shape, 0)
    rank = jnp.zeros(gm.shape, F32)
    for j in range(n_blocks):
        other = gm[j:j + 1, :]
        ahead = (other > gm) | ((other == gm) & (j < row))
        rank = rank + jnp.where(ahead, 1.0, 0.0)
    return rank < float(MOBA_TOPK)


def _moba_prompt_kernel(slopes_ref, q_ref, k_ref, v_ref, o_ref, *, seq):
    nb = seq // MOBA_BLOCK
    qrows = MOBA_BLOCK
    heads = LANES // MOBA_DH
    scale = MOBA_DH ** -0.5
    hp = pl.program_id(1)
    lane_q = lax.broadcasted_iota(jnp.int32, (qrows, LANES), 1)
    rel0 = (lax.broadcasted_iota(jnp.int32, (qrows, MOBA_BLOCK), 0)
            - lax.broadcasted_iota(jnp.int32, (qrows, MOBA_BLOCK), 1))
    causal = rel0 >= 0
    e_rows = lax.broadcasted_iota(jnp.int32, (LANES, MOBA_BLOCK), 0)
    nb_pad = -(-nb // SUBLANES) * SUBLANES
    assert nb_pad <= LANES
    blk_t = lax.broadcasted_iota(jnp.int32, (nb_pad, qrows), 0)
    hmask = [lane_q // MOBA_DH == x for x in range(heads)]
    slopes = [slopes_ref[hp * heads + x] for x in range(heads)]
    srel = [slopes[x] * rel0.astype(F32) for x in range(heads)]
    kmean = jnp.concatenate(
        [jnp.mean(k_ref[0, j * MOBA_BLOCK:(j + 1) * MOBA_BLOCK, :], axis=0, keepdims=True)
         for j in range(nb)] + [jnp.zeros((LANES - nb, LANES), F32)], axis=0)

    def q_block(cur, carry):
        r0 = pl.multiple_of(cur * qrows, qrows)
        q = q_ref[0, pl.ds(r0, qrows), :]
        past_t = blk_t < cur
        qx, sel = [], []
        for x in range(heads):
            qh = jnp.where(hmask[x], q, 0.0)
            gate_t = lax.dot_general(kmean, qh, _NT, preferred_element_type=F32,
                                     precision=lax.Precision.HIGHEST)[:nb_pad, :]
            keep_t = past_t & _topk_mask_t(jnp.where(past_t, gate_t, NEG), nb)
            sel_t = jnp.concatenate([jnp.where(keep_t, 1.0, 0.0), jnp.zeros((LANES - nb_pad, qrows), F32)], axis=0)
            sel.append(sel_t.T.astype(BF16))
            qx.append(qh.astype(BF16))

        kb = k_ref[0, pl.ds(r0, qrows), :].astype(BF16)
        vb = v_ref[0, pl.ds(r0, qrows), :].astype(BF16)
        init = []
        for x in range(heads):
            s = lax.dot_general(qx[x], kb, _NT, preferred_element_type=F32) * scale - srel[x]
            s = jnp.where(causal, s, NEG)
            m = jnp.max(s, axis=-1, keepdims=True)
            p = jnp.exp(s - m)
            init.append((m, jnp.sum(p, axis=-1, keepdims=True),
                         jnp.dot(p.astype(BF16), vb, preferred_element_type=F32)))

        def past_block(j, st):
            k0 = pl.multiple_of(j * MOBA_BLOCK, MOBA_BLOCK)
            kb = k_ref[0, pl.ds(k0, MOBA_BLOCK), :].astype(BF16)
            vb = v_ref[0, pl.ds(k0, MOBA_BLOCK), :].astype(BF16)
            ej = jnp.where(e_rows == j, 1.0, 0.0).astype(BF16)
            off = (r0 - k0).astype(F32)
            out = []
            for x in range(heads):
                m, l, acc = st[x]
                selb = jnp.dot(sel[x], ej, preferred_element_type=F32)
                s = lax.dot_general(qx[x], kb, _NT, preferred_element_type=F32) * scale
                s = s - (srel[x] + slopes[x] * off)
                s = jnp.where(selb > 0.5, s, NEG)
                m_new = jnp.maximum(m, jnp.max(s, axis=-1, keepdims=True))
                alpha = jnp.exp(m - m_new)
                p = jnp.exp(s - m_new)
                acc = alpha * acc + jnp.dot(p.astype(BF16), vb, preferred_element_type=F32)
                out.append((m_new, alpha * l + jnp.sum(p, axis=-1, keepdims=True), acc))
            return tuple(out)

        st = lax.fori_loop(0, cur, past_block, tuple(init))
        o = st[heads - 1][2] / st[heads - 1][1]
        for x in range(heads - 2, -1, -1):
            o = jnp.where(hmask[x], st[x][2] / st[x][1], o)
        o_ref[0, pl.ds(r0, qrows), :] = o.astype(o_ref.dtype)
        return carry

    lax.fori_loop(0, nb, q_block, 0)


def _moba_prompt(qb, kb, vb):
    bsz, t, _ = qb.shape
    blk = pl.BlockSpec((1, t, LANES), lambda b, h: (b, 0, h))
    return pl.pallas_call(
        functools.partial(_moba_prompt_kernel, seq=t),
        grid=(bsz, MOBA_W // LANES),
        in_specs=[pl.BlockSpec(memory_space=pltpu.SMEM), blk, blk, blk],
        out_specs=blk,
        out_shape=jax.ShapeDtypeStruct((bsz, t, MOBA_W), BF16),
        compiler_params=_cparams(("parallel", "parallel")),
        name="moba_prompt",
    )(_alibi_slopes(), qb, kb, vb)


def _moba_sample_kernel(pt_ref, slopes_ref, q_ref, kn_ref, vn_ref, *rest, n_new, n_past_blocks, past_len,
                        blocks_per_step, pages_per_block):
    del pt_ref
    n_pg = blocks_per_step * pages_per_block
    k_refs, v_refs = rest[:n_pg], rest[n_pg:2 * n_pg]
    o_ref, qbd, gate_s, m_s, l_s, o_s = rest[2 * n_pg:]
    j = pl.program_id(1)
    rows = n_new * MOBA_HEADS
    page = k_refs[0].shape[2]
    scale = MOBA_DH ** -0.5
    row_i = lax.broadcasted_iota(jnp.int32, (rows, 1), 0)
    head_r = row_i % MOBA_HEADS
    tok_r = row_i // MOBA_HEADS
    slope = jnp.zeros((rows, 1), F32)
    for h in range(MOBA_HEADS):
        slope = jnp.where(head_r == h, slopes_ref[h], slope)
    head_mask = (lax.broadcasted_iota(jnp.int32, (rows, MOBA_W), 1) // MOBA_DH
                 == lax.broadcasted_iota(jnp.int32, (rows, MOBA_W), 0) % MOBA_HEADS)
    key_i = lax.broadcasted_iota(jnp.int32, (rows, page), 1)
    lane_nb = lax.broadcasted_iota(jnp.int32, (rows, LANES), 1)

    @pl.when(j == 0)
    def _():
        q = q_ref[0]
        qrep = jnp.concatenate([jnp.broadcast_to(q[t:t + 1, :], (MOBA_HEADS, MOBA_W))
                                for t in range(n_new)], axis=0)
        qbd[...] = jnp.where(head_mask, qrep, 0.0)
        gate_s[...] = jnp.full(gate_s.shape, NEG, F32)
        m_s[...] = jnp.full(m_s.shape, NEG, F32)
        l_s[...] = jnp.zeros(l_s.shape, F32)

    qf = qbd[...]
    q16 = qf.astype(BF16)
    gate_all, m_all, l_all = gate_s[...], m_s[...], l_s[...]
    for bi in range(blocks_per_step):
        blk = j * blocks_per_step + bi
        ss = []
        gate_j = jnp.zeros((rows, 1), F32)
        for o in range(pages_per_block):
            kt = k_refs[bi * pages_per_block + o][0].astype(BF16)
            raw = jnp.dot(q16, kt, preferred_element_type=F32)
            gate_j = gate_j + jnp.sum(raw, axis=-1, keepdims=True)
            dist = (past_len - blk * MOBA_BLOCK - o * page) + tok_r - key_i
            ss.append(raw * scale - slope * dist.astype(F32))
        gate_j = gate_j * (1.0 / MOBA_BLOCK)
        m = jnp.max(ss[0], axis=-1, keepdims=True)
        for s in ss[1:]:
            m = jnp.maximum(m, jnp.max(s, axis=-1, keepdims=True))
        l = jnp.zeros((rows, 1), F32)
        o_blk = jnp.zeros((rows, MOBA_W), F32)
        for o, s in enumerate(ss):
            p = jnp.exp(s - m)
            l = l + jnp.sum(p, axis=-1, keepdims=True)
            vt = v_refs[bi * pages_per_block + o][0].astype(BF16)
            o_blk = o_blk + lax.dot_general(p.astype(BF16), vt, _NT, preferred_element_type=F32)
        here = lane_nb == blk
        gate_all = jnp.where(here, gate_j, gate_all)
        m_all = jnp.where(here, m, m_all)
        l_all = jnp.where(here, l, l_all)
        o_s[blk] = o_blk
    gate_s[...] = gate_all
    m_s[...] = m_all
    l_s[...] = l_all

    @pl.when(j == pl.num_programs(1) - 1)
    def _():
        sel = _topk_mask(gate_s[...], n_past_blocks) & (lane_nb < n_past_blocks)
        kn = kn_ref[0]
        vn = vn_ref[0]
        s_own = []
        for c in range(n_new):
            sc = jnp.sum(qf * kn[c:c + 1, :], axis=-1, keepdims=True) * scale
            sc = sc - slope * (tok_r - c).astype(F32)
            s_own.append(jnp.where(tok_r >= c, sc, NEG))
        m_sel = jnp.where(sel, m_s[...], NEG)
        m_tot = jnp.max(m_sel, axis=-1, keepdims=True)
        for sc in s_own:
            m_tot = jnp.maximum(m_tot, sc)
        w = jnp.where(sel, jnp.exp(m_sel - m_tot), 0.0)
        denom = jnp.sum(w * l_s[...], axis=-1, keepdims=True)
        num = jnp.zeros((rows, MOBA_W), F32)
        for c, sc in enumerate(s_own):
            pc = jnp.exp(sc - m_tot)
            denom = denom + pc
            num = num + pc * vn[c:c + 1, :]

        def add_block(jj, acc):
            wj = jnp.sum(jnp.where(lane_nb == jj, w, 0.0), axis=-1, keepdims=True)
            return acc + wj * o_s[jj]

        num = lax.fori_loop(0, n_past_blocks, add_block, num)
        res = jnp.where(head_mask, num / denom, 0.0)
        o_ref[0] = jnp.concatenate(
            [jnp.sum(res[t * MOBA_HEADS:(t + 1) * MOBA_HEADS, :], axis=0, keepdims=True)
             for t in range(n_new)], axis=0).astype(o_ref.dtype)


MOBA_SAMPLE_BLOCKS_PER_STEP = 4


def _moba_sample(qb, kb, vb, cache_k, cache_v, page_table):
    bsz, n_new, _ = qb.shape
    n_pages = page_table.shape[1]
    page = cache_k.shape[1]
    ppb = MOBA_BLOCK // page
    assert ppb * page == MOBA_BLOCK and n_pages % ppb == 0
    nbp = n_pages // ppb
    bps = MOBA_SAMPLE_BLOCKS_PER_STEP if nbp % MOBA_SAMPLE_BLOCKS_PER_STEP == 0 else 1
    assert nbp <= LANES
    rows = n_new * MOBA_HEADS
    to_pages = lambda c: jnp.transpose(c, (0, 2, 3, 1)).reshape(c.shape[0], MOBA_W, page)
    new = pl.BlockSpec((1, n_new, MOBA_W), lambda b, j, pt: (b, 0, 0))
    n_pg = bps * ppb
    pg = lambda o: pl.BlockSpec((1, MOBA_W, page), lambda b, j, pt: (pt[b, n_pg * j + o], 0, 0))
    grid_spec = pltpu.PrefetchScalarGridSpec(
        num_scalar_prefetch=1,
        grid=(bsz, nbp // bps),
        in_specs=[pl.BlockSpec(memory_space=pltpu.SMEM), new, new, new]
                 + [pg(o) for o in range(n_pg)] + [pg(o) for o in range(n_pg)],
        out_specs=new,
        scratch_shapes=[pltpu.VMEM((rows, MOBA_W), F32), pltpu.VMEM((rows, LANES), F32),
                        pltpu.VMEM((rows, LANES), F32), pltpu.VMEM((rows, LANES), F32),
                        pltpu.VMEM((nbp, rows, MOBA_W), F32)],
    )
    ck, cv = to_pages(cache_k), to_pages(cache_v)
    return pl.pallas_call(
        functools.partial(_moba_sample_kernel, n_new=n_new, n_past_blocks=nbp, past_len=n_pages * page,
                          blocks_per_step=bps, pages_per_block=ppb),
        grid_spec=grid_spec,
        out_shape=jax.ShapeDtypeStruct((bsz, n_new, MOBA_W), BF16),
        compiler_params=_cparams(("parallel", "arbitrary")),
        name="moba_sample",
    )(page_table, _alibi_slopes(), qb, kb, vb, *([ck] * n_pg), *([cv] * n_pg))


def _merge_kernel(x_ref, oa_ref, ob_ref, ga_ref, gb_ref, wpa, wpb, wout, n2, wq, keys,
                  x1_ref, xnt_ref, st_ref):
    ya = jnp.dot(oa_ref[...], wpa[...], preferred_element_type=F32)
    yb = jnp.dot(ob_ref[...], wpb[...], preferred_element_type=F32)
    merged = jax.nn.sigmoid(ga_ref[...]) * ya + jax.nn.sigmoid(gb_ref[...]) * yb
    x1 = x_ref[...] + jnp.dot(merged.astype(BF16), wout[...], preferred_element_type=F32)
    x1_ref[...] = x1
    xn = _rms(x1, n2[...])
    xnt_ref[...] = xn.T.astype(BF16)
    q = jnp.dot(xn.astype(BF16), wq[...], preferred_element_type=F32).astype(BF16)
    for i in range(2 * PEER_HEADS):
        st_ref[i * PEER_NKEYS:(i + 1) * PEER_NKEYS, :] = lax.dot_general(
            keys[i], q[:, i * PEER_HALF:(i + 1) * PEER_HALF], _NT, preferred_element_type=F32)


def _merge(x2, oa, ob, ga, gb, w_pa, w_pb, w_out, n2, wq, keys, tm):
    n = x2.shape[0]
    ws = [w_pa.astype(BF16), w_pb.astype(BF16), w_out.astype(BF16), n2.reshape(1, -1), wq.astype(BF16),
          keys.astype(BF16).reshape(2 * PEER_HEADS, PEER_NKEYS, PEER_HALF)]
    row = lambda w: pl.BlockSpec((tm, w), lambda i: (i, 0))
    full = lambda a: pl.BlockSpec(a.shape, lambda i: (0,) * a.ndim)
    n_s = 2 * PEER_HEADS * PEER_NKEYS
    return pl.pallas_call(
        _merge_kernel,
        grid=(n // tm,),
        in_specs=[row(D_MODEL), row(GLA_V_W), row(MOBA_W), row(D_MODEL), row(D_MODEL)] + [full(w) for w in ws],
        out_specs=[row(D_MODEL), pl.BlockSpec((D_MODEL, tm), lambda i: (0, i)),
                   pl.BlockSpec((n_s, tm), lambda i: (0, i))],
        out_shape=[jax.ShapeDtypeStruct((n, D_MODEL), F32), jax.ShapeDtypeStruct((D_MODEL, n), BF16),
                   jax.ShapeDtypeStruct((n_s, n), F32)],
        compiler_params=_cparams(("parallel",)),
        name="merge",
    )(x2, oa, ob, ga, gb, *ws)


def _extract_top(vals, n):
    out = []
    for i in range(n):
        m = jnp.max(vals, axis=0, keepdims=True)
        out.append(m)
        if i + 1 < n:
            vals = jnp.where(vals == m, -jnp.inf, vals)
    return out


def _peer_select_kernel(st_ref, e1_ref, e2_ref, tau_ref, *, lane_tiles):
    def head(idx, carry):
        h = idx // lane_tiles
        ls = pl.ds(pl.multiple_of((idx % lane_tiles) * LANES, LANES), LANES)
        r1 = pl.multiple_of(h * 2 * PEER_NKEYS, 2 * PEER_NKEYS)
        r2 = pl.multiple_of(h * 2 * PEER_NKEYS + PEER_NKEYS, PEER_NKEYS)
        ro = pl.multiple_of(h * PEER_NKEYS, PEER_NKEYS)
        s1 = st_ref[pl.ds(r1, PEER_NKEYS), ls]
        s2 = st_ref[pl.ds(r2, PEER_NKEYS), ls]
        v1 = _extract_top(s1, PEER_TOPK)
        v2 = _extract_top(s2, PEER_TOPK)
        v1a = jnp.concatenate(v1, axis=0)
        v2a = jnp.concatenate(v2, axis=0)
        cands = [v1[0] + v2a, v1[1] + v2a[:8], v1[2] + v2a[:8], v1[3] + v2a[:8]]
        cands += [v1[i] + v2a[:8] for i in range(4, 8)]
        cands.append(v1a[8:] + v2[0])
        cand = jnp.concatenate(cands, axis=0)
        best = _extract_top(cand, PEER_TOPK)
        z = jnp.ones_like(best[0])
        for bk in best[1:]:
            z = z + jnp.exp(bk - best[0])
        e1_ref[pl.ds(ro, PEER_NKEYS), ls] = jnp.exp(s1 - v1[0]) / z
        e2_ref[pl.ds(ro, PEER_NKEYS), ls] = jnp.exp(s2 - v2[0])
        tau_ref[pl.ds(pl.multiple_of(h * SUBLANES, SUBLANES), SUBLANES), ls] = jnp.broadcast_to(
            best[-1], (SUBLANES, LANES))
        return carry

    lax.fori_loop(0, PEER_HEADS * lane_tiles, head, 0)


def _peer_select(st, tn):
    n_s, n = st.shape
    n_e = PEER_HEADS * PEER_NKEYS
    return pl.pallas_call(
        functools.partial(_peer_select_kernel, lane_tiles=tn // LANES),
        grid=(n // tn,),
        in_specs=[pl.BlockSpec((n_s, tn), lambda i: (0, i))],
        out_specs=[pl.BlockSpec((n_e, tn), lambda i: (0, i)), pl.BlockSpec((n_e, tn), lambda i: (0, i)),
                   pl.BlockSpec((PEER_HEADS * SUBLANES, tn), lambda i: (0, i))],
        out_shape=[jax.ShapeDtypeStruct((n_e, n), F32), jax.ShapeDtypeStruct((n_e, n), F32),
                   jax.ShapeDtypeStruct((PEER_HEADS * SUBLANES, n), F32)],
        compiler_params=_cparams(("parallel",)),
        name="peer_select",
    )(st)


def _zero_from(x):
    u = lax.bitcast_convert_type(x, jnp.uint32)
    u = lax.shift_right_logical(lax.shift_right_logical(u, jnp.uint32(16)), jnp.uint32(16))
    return lax.bitcast_convert_type(u, F32)


def _peer_dense_kernel(xnt_ref, x1_ref, st_ref, e1_ref, e2_ref, tau_ref, u_ref, vt_ref, fg_ref, y_ref,
                       acc, hbuf0, hbuf1, act0, act1, *, eb, tn, n_blocks):
    jj = pl.program_id(1)
    hbuf, act = (hbuf0, hbuf1), (act0, act1)
    groups = eb // PEER_NKEYS
    assert groups == SUBLANES, "one aligned 8-row load of first-key scores per expert block"
    inv_sqrt2 = 1.0 / math.sqrt(2.0)

    tiles_per_it = math.gcd(PEER_LANE_TILES_PER_ITER, tn // LANES)
    n_it = tn // (tiles_per_it * LANES)
    h_rows = eb // n_it
    o_rows = D_MODEL // n_it

    n_strips_tile = PEER_NKEYS // PEER_KEY_STRIP

    def tile_rows(ls):
        a0 = (jj - 1) * SUBLANES
        s1b = [st_ref[pl.ds(pl.multiple_of(h * 2 * PEER_NKEYS + a0, SUBLANES), SUBLANES), ls]
               for h in range(PEER_HEADS)]
        e1b = [e1_ref[pl.ds(pl.multiple_of(h * PEER_NKEYS + a0, SUBLANES), SUBLANES), ls]
               for h in range(PEER_HEADS)]
        taus = [tau_ref[h * SUBLANES:h * SUBLANES + 1, ls] for h in range(PEER_HEADS)]
        return s1b, e1b, taus

    def mixture_strip(slot, ls, rows, si, pin):
        s1b, e1b, taus = rows
        b0 = si * PEER_KEY_STRIP
        w = [jnp.zeros((PEER_KEY_STRIP, LANES), F32) for _ in range(groups)]
        for h in range(PEER_HEADS):
            r2 = h * 2 * PEER_NKEYS + PEER_NKEYS + b0
            s2 = st_ref[r2:r2 + PEER_KEY_STRIP, ls]
            e2 = e2_ref[h * PEER_NKEYS + b0:h * PEER_NKEYS + b0 + PEER_KEY_STRIP, ls]
            for ai in range(groups):
                keep = s1b[h][ai:ai + 1, :] + s2 >= taus[h]
                w[ai] = w[ai] + jnp.where(keep, e2 * e1b[h][ai:ai + 1, :], 0.0)
        for ai in range(groups):
            es = slice(ai * PEER_NKEYS + b0, ai * PEER_NKEYS + b0 + PEER_KEY_STRIP)
            h_t = hbuf[slot][es, ls]
            gelu = 0.5 * h_t * (1.0 + lax.erf(h_t * inv_sqrt2))
            val = w[ai] * gelu
            if ai == groups - 1:
                val = val + jnp.concatenate([pin] * (PEER_KEY_STRIP // SUBLANES), axis=0)
            act[slot][es, ls] = val.astype(BF16)

    def sweep(slot, do_hidden, do_mixture, do_output):
        n_strips = tiles_per_it * n_strips_tile
        half = n_strips // 2
        ch_h, ch_o = h_rows // half, o_rows // half

        def step(it, carry):
            h0 = pl.multiple_of(it * h_rows, h_rows)
            o0 = pl.multiple_of(it * o_rows, o_rows)
            r_h = r_o = None
            if do_hidden:
                r_h = jnp.dot(u_ref[pl.ds(h0, h_rows), :], xnt_ref[...], preferred_element_type=F32)
            if do_output:
                r_o = jnp.dot(vt_ref[pl.ds(o0, o_rows), :], act[1 - slot][...], preferred_element_type=F32)
            if not do_mixture:
                if do_hidden:
                    hbuf[1 - slot][pl.ds(h0, h_rows), :] = r_h
                if do_output:
                    acc[pl.ds(o0, o_rows), :] += r_o
                return carry
            for idx in range(n_strips):
                k, si = divmod(idx, n_strips_tile)
                ls = pl.ds(pl.multiple_of((it * tiles_per_it + k) * LANES, LANES), LANES)
                if si == 0:
                    rows = tile_rows(ls)
                if idx < half:
                    c = idx * ch_h
                    mixture_strip(slot, ls, rows, si, _zero_from(r_h[c:c + SUBLANES, 0:LANES]))
                    hbuf[1 - slot][pl.ds(pl.multiple_of(h0 + c, ch_h), ch_h), :] = r_h[c:c + ch_h, :]
                else:
                    c = (idx - half) * ch_o
                    mixture_strip(slot, ls, rows, si, _zero_from(r_o[c:c + SUBLANES, 0:LANES]))
                    acc[pl.ds(pl.multiple_of(o0 + c, ch_o), ch_o), :] += r_o[c:c + ch_o, :]
            return carry
        lax.fori_loop(0, n_it, step, 0)

    @pl.when(jj == 0)
    def _():
        acc[...] = jnp.zeros(acc.shape, F32)
        act[0][...] = jnp.zeros(act[0].shape, BF16)
        sweep(0, True, False, False)

    for slot in range(2):
        @pl.when((jj >= 1) & (jj <= n_blocks) & (jj % 2 == slot))
        def _(slot=slot):
            sweep(slot, True, True, True)

    @pl.when(jj == n_blocks + 1)
    def _():
        sweep((n_blocks + 1) % 2, False, False, True)
        y = x1_ref[...] + acc[...].T
        y_ref[...] = _rms(y, fg_ref[...])


def _peer_dense(xnt, x1, st, e1, e2, tau, u_bf, vt_bf, final_g, tn, eb):
    n = x1.shape[0]
    n_blocks = PEER_N // eb
    tok = lambda w: pl.BlockSpec((tn, w), lambda i, j: (i, 0))
    col = lambda a: pl.BlockSpec((a.shape[0], tn), lambda i, j: (0, i))
    return pl.pallas_call(
        functools.partial(_peer_dense_kernel, eb=eb, tn=tn, n_blocks=n_blocks),
        grid=(n // tn, n_blocks + 2),
        in_specs=[col(xnt), tok(D_MODEL), col(st), col(e1), col(e2), col(tau),
                  pl.BlockSpec((eb, D_MODEL), lambda i, j: (jnp.minimum(j, n_blocks - 1), 0)),
                  pl.BlockSpec((D_MODEL, eb), lambda i, j: (0, jnp.clip(j - 2, 0, n_blocks - 1))),
                  pl.BlockSpec((1, D_MODEL), lambda i, j: (0, 0))],
        out_specs=tok(D_MODEL),
        out_shape=jax.ShapeDtypeStruct((n, D_MODEL), F32),
        scratch_shapes=[pltpu.VMEM((D_MODEL, tn), F32), pltpu.VMEM((eb, tn), F32), pltpu.VMEM((eb, tn), F32),
                        pltpu.VMEM((eb, tn), BF16), pltpu.VMEM((eb, tn), BF16)],
        compiler_params=_cparams(("parallel", "arbitrary")),
        name="peer_dense",
    )(xnt, x1, st, e1, e2, tau, u_bf, vt_bf, final_g.reshape(1, -1))


def _group(x, s0, moba_fn, wts, tm, tn, eb):
    (n1, w_in, wa2, ba, gla_g, w_pa, w_pb, w_out, n2, wq, keys, u_bf, vt_bf, final_g) = wts
    bsz, t, _ = x.shape
    n = bsz * t
    x2 = x.reshape(n, D_MODEL)
    qa, ka, va, ra, la, qb, kb, vb, ga, gb = _in_proj(x2, n1, w_in, wa2, ba, tm)
    r3 = lambda a: a.reshape(bsz, t, a.shape[-1])
    oa, s_fin = _gla(r3(qa), r3(ka), r3(va), r3(la), r3(ra), s0, gla_g)
    ob = moba_fn(r3(qb), r3(kb), r3(vb))
    x1, xn, st = _merge(x2, oa.reshape(n, GLA_V_W), ob.reshape(n, MOBA_W), ga, gb,
                        w_pa, w_pb, w_out, n2, wq, keys, tm)
    e1, e2, tau = _peer_select(st, tn)
    y = _peer_dense(xn, x1, st, e1, e2, tau, u_bf, vt_bf, final_g, tn, eb)
    kv_shape = (1, bsz, t, MOBA_HEADS, MOBA_DH)
    return y.reshape(bsz, t, D_MODEL), kb.reshape(kv_shape), vb.reshape(kv_shape), s_fin[None]


def kernel(x_prompt, x_sample, cache_k, cache_v, state_gla, page_table, norm1_g, w_in, gla_wa2, gla_ba,
           gla_norm_g, w_pa, w_pb, w_out, norm2_g, peer_wq, peer_keys, peer_u, peer_v, final_g):
    assert w_in.shape[0] == 1, "single-layer step"
    u_bf = peer_u[0].astype(BF16)
    vt_bf = peer_v[0].astype(BF16).T
    wts = (norm1_g[0], w_in[0], gla_wa2[0], gla_ba[0], gla_norm_g[0], w_pa[0], w_pb[0], w_out[0],
           norm2_g[0], peer_wq[0], peer_keys[0], u_bf, vt_bf, final_g)
    bp = x_prompt.shape[0]
    s0p = jnp.zeros((bp, GLA_HEADS, GLA_DK, GLA_DV), F32)
    n_s = x_sample.shape[0] * x_sample.shape[1]
    tok_blk = lambda n, pref: pref if n % pref == 0 else n
    n_p = bp * x_prompt.shape[1]
    yp, kp, vp, sp = _group(x_prompt, s0p, _moba_prompt, wts,
                            tok_blk(n_p, 256), tok_blk(n_p, 512), PEER_EXPERT_BLOCK)
    ck, cv = cache_k[0], cache_v[0]
    ys, ks, vs, ss = _group(x_sample, state_gla[0],
                            lambda q, k, v: _moba_sample(q, k, v, ck, cv, page_table), wts,
                            tok_blk(n_s, 256), tok_blk(n_s, 512), PEER_EXPERT_BLOCK)
    return (yp, ys, kp, vp, sp, ks, vs, ss)
```

```python
import functools
import math

import jax
import jax.numpy as jnp
from jax import lax
from jax.experimental import pallas as pl
from jax.experimental.pallas import tpu as pltpu

F32 = jnp.float32
BF16 = jnp.bfloat16

D_MODEL = 1024
GLA_HEADS = 4
GLA_DK = 64
GLA_DV = 128
GLA_RANK = 16
GLA_TAU = 16.0
GLA_CHUNK = 64
GLA_QK_W = GLA_HEADS * GLA_DK
GLA_V_W = GLA_HEADS * GLA_DV
MOBA_HEADS = 8
MOBA_DH = 64
MOBA_W = MOBA_HEADS * MOBA_DH
MOBA_BLOCK = 256
MOBA_TOPK = 3
MOBA_QCHUNK = 128
NEG = -1e30
PEER_HEADS = 8
PEER_NKEYS = 128
PEER_N = PEER_NKEYS * PEER_NKEYS
PEER_QDIM = 256
PEER_HALF = PEER_QDIM // 2
PEER_TOPK = 16
EPS = 1e-6
IN_SIZES = (GLA_QK_W, GLA_QK_W, GLA_V_W, GLA_V_W, GLA_RANK, MOBA_W, MOBA_W, MOBA_W, D_MODEL, D_MODEL)

LANES = 128
SUBLANES = 8
PEER_EXPERT_BLOCK = SUBLANES * PEER_NKEYS
PEER_KEY_STRIP = 16
PEER_LANE_TILES_PER_ITER = 2
VMEM_LIMIT = 56 * 1024 * 1024

_NT = (((1,), (1,)), ((), ()))
_TN = (((0,), (0,)), ((), ()))


def _cparams(sem):
    return pltpu.CompilerParams(dimension_semantics=sem, vmem_limit_bytes=VMEM_LIMIT)


def _rms(x, g):
    return x * lax.rsqrt(jnp.mean(x * x, axis=-1, keepdims=True) + EPS) * g


def _inproj_kernel(x_ref, g_ref, wqa, wka, wva, wra, wlr, wa2, ba, wqb, wkb, wvb, wga, wgb,
                   qa, ka, va, ra, la, qb, kb, vb, ga, gb):
    xn = _rms(x_ref[...], g_ref[...]).astype(BF16)
    for w, o in ((wqa, qa), (wka, ka), (wva, va), (wra, ra), (wqb, qb), (wkb, kb), (wvb, vb),
                 (wga, ga), (wgb, gb)):
        o[...] = jnp.dot(xn, w[...], preferred_element_type=F32).astype(o.dtype)
    lr = jnp.dot(xn, wlr[...], preferred_element_type=F32).astype(BF16)
    z = jnp.dot(lr, wa2[...], preferred_element_type=F32) + ba[...]
    la[...] = (jnp.minimum(z, 0.0) - jnp.log1p(jnp.exp(-jnp.abs(z)))) * (1.0 / GLA_TAU)


def _in_proj(x2, n1, w_in, wa2, ba, tm):
    n = x2.shape[0]
    offs = [0]
    for c in IN_SIZES:
        offs.append(offs[-1] + c)
    wb = w_in.astype(BF16)
    piece = lambda i: wb[:, offs[i]:offs[i + 1]]
    wlr = jnp.pad(piece(4), ((0, 0), (0, LANES - GLA_RANK)))
    wa2p = jnp.pad(wa2.astype(BF16), ((0, LANES - GLA_RANK), (0, 0)))
    weights = [piece(0), piece(1), piece(2), piece(3), wlr, wa2p, ba.reshape(1, -1),
               piece(5), piece(6), piece(7), piece(8), piece(9)]
    widths = [GLA_QK_W, GLA_QK_W, GLA_V_W, GLA_V_W, GLA_QK_W, MOBA_W, MOBA_W, MOBA_W, D_MODEL, D_MODEL]
    row = lambda w: pl.BlockSpec((tm, w), lambda i: (i, 0))
    full = lambda a: pl.BlockSpec(a.shape, lambda i: (0, 0))
    return pl.pallas_call(
        _inproj_kernel,
        grid=(n // tm,),
        in_specs=[row(D_MODEL), full(n1.reshape(1, -1))] + [full(w) for w in weights],
        out_specs=[row(w) for w in widths],
        out_shape=[jax.ShapeDtypeStruct((n, w), F32) for w in widths],
        compiler_params=_cparams(("parallel",)),
        name="in_proj",
    )(x2, n1.reshape(1, -1), *weights)


def _gla_kernel(qa, ka, va, la, ra, s0, g_ref, oa, sfin, s_scr, *, chunk, n_chunks, seqs):
    t = pl.program_id(1)

    @pl.when(t == 0)
    def _():
        s_scr[...] = s0[...]

    ri = lax.broadcasted_iota(jnp.int32, (chunk, chunk), 0)
    ci = lax.broadcasted_iota(jnp.int32, (chunk, chunk), 1)
    causal = ci <= ri
    tril = causal.astype(F32)
    eye_k = (lax.broadcasted_iota(jnp.int32, (GLA_DK, GLA_DK), 0)
             == lax.broadcasted_iota(jnp.int32, (GLA_DK, GLA_DK), 1))
    ones_kv = jnp.ones((GLA_DK, GLA_DV), F32)
    g = g_ref[...]

    def body(c, carry):
        r0 = pl.multiple_of(c * chunk, chunk)
        rows = pl.ds(r0, chunk)
        for bi in range(seqs):
            a = la[bi, rows, :]
            b = jnp.dot(tril, a, preferred_element_type=F32, precision=lax.Precision.HIGHEST)
            b_last = b[chunk - 1:chunk, :]
            q = qa[bi, rows, :] * (GLA_DK ** -0.5)
            k = ka[bi, rows, :]
            q_dec = (q * jnp.exp(b)).astype(BF16)
            k_inv = (k * jnp.exp(-b)).astype(BF16)
            k_up = (k * jnp.exp(b_last - b)).astype(BF16)
            dec = jnp.exp(b_last)
            outs = []
            for h in range(GLA_HEADS):
                ks = slice(h * GLA_DK, (h + 1) * GLA_DK)
                vs = slice(h * GLA_DV, (h + 1) * GLA_DV)
                s = s_scr[bi, h]
                v = va[bi, rows, vs].astype(BF16)
                o = jnp.dot(q_dec[:, ks], s.astype(BF16), preferred_element_type=F32)
                att = lax.dot_general(q_dec[:, ks], k_inv[:, ks], _NT, preferred_element_type=F32)
                att = jnp.where(causal, att, 0.0).astype(BF16)
                o = o + jnp.dot(att, v, preferred_element_type=F32)
                kv = lax.dot_general(k_up[:, ks], v, _TN, preferred_element_type=F32)
                dmat = jnp.where(eye_k, jnp.broadcast_to(dec[:, ks], (GLA_DK, GLA_DK)), 0.0)
                dcol = jnp.dot(dmat, ones_kv, preferred_element_type=F32, precision=lax.Precision.HIGHEST)
                s_scr[bi, h] = dcol * s + kv
                on = _rms(o, g)
                r = ra[bi, rows, vs]
                outs.append(on * (r * jax.nn.sigmoid(r)))
            oa[bi, rows, :] = jnp.concatenate(outs, axis=-1).astype(oa.dtype)
        return carry

    lax.fori_loop(0, n_chunks, body, 0)

    @pl.when(t == pl.num_programs(1) - 1)
    def _():
        sfin[...] = s_scr[...]


GLA_MIN_CHUNK = 16
GLA_SEQS_PER_STEP = 2


def _gla(qa, ka, va, la, ra, s0, gla_g):
    bsz, t_real, _ = qa.shape
    chunk = math.gcd(GLA_CHUNK, t_real)
    if chunk < GLA_MIN_CHUNK:
        chunk = GLA_MIN_CHUNK
        pad = (-t_real) % chunk
        qa, ka, va, la, ra = (jnp.pad(a, ((0, 0), (0, pad), (0, 0))) for a in (qa, ka, va, la, ra))
    t = qa.shape[1]
    tb = min(t, 512)
    seqs = math.gcd(GLA_SEQS_PER_STEP, bsz)
    blk = lambda w: pl.BlockSpec((seqs, tb, w), lambda b, i: (b, i, 0))
    st = pl.BlockSpec((seqs, GLA_HEADS, GLA_DK, GLA_DV), lambda b, i: (b, 0, 0, 0))
    oa, s_fin = pl.pallas_call(
        functools.partial(_gla_kernel, chunk=chunk, n_chunks=tb // chunk, seqs=seqs),
        grid=(bsz // seqs, t // tb),
        in_specs=[blk(GLA_QK_W), blk(GLA_QK_W), blk(GLA_V_W), blk(GLA_QK_W), blk(GLA_V_W), st,
                  pl.BlockSpec((1, GLA_DV), lambda b, i: (0, 0))],
        out_specs=[blk(GLA_V_W), st],
        out_shape=[jax.ShapeDtypeStruct((bsz, t, GLA_V_W), BF16),
                   jax.ShapeDtypeStruct((bsz, GLA_HEADS, GLA_DK, GLA_DV), F32)],
        scratch_shapes=[pltpu.VMEM((seqs, GLA_HEADS, GLA_DK, GLA_DV), F32)],
        compiler_params=_cparams(("parallel", "arbitrary")),
        name="gla",
    )(qa, ka, va, la, ra, s0, gla_g.reshape(1, -1))
    return oa[:, :t_real], s_fin


def _alibi_slopes():
    return jnp.exp2(-8.0 * jnp.arange(1, MOBA_HEADS + 1, dtype=F32) / MOBA_HEADS)


def _topk_mask(gm, n_blocks):
    lane = lax.broadcasted_iota(jnp.int32, gm.shape, 1)
    rank = jnp.zeros(gm.shape, F32)
    for j in range(n_blocks):
        col = gm[:, j:j + 1]
        ahead = (col > gm) | ((col == gm) & (j < lane))
        rank = rank + jnp.where(ahead, 1.0, 0.0)
    return rank < float(MOBA_TOPK)


def _topk_mask_t(gm, n_blocks):
    row = lax.broadcasted_iota(jnp.int32, gm.shape, 0)
    rank = jnp.zeros(gm.shape, F32)
    for j in range(n_blocks):
        other = gm[j:j + 1, :]
        ahead = (other > gm) | ((other == gm) & (j < row))
        rank = rank + jnp.where(ahead, 1.0, 0.0)
    return rank < float(MOBA_TOPK)


def _moba_prompt_kernel(slopes_ref, q_ref, k_ref, v_ref, o_ref, *, seq):
    nb = seq // MOBA_BLOCK
    qrows = MOBA_BLOCK
    heads = LANES // MOBA_DH
    scale = MOBA_DH ** -0.5
    hp = pl.program_id(1)
    lane_q = lax.broadcasted_iota(jnp.int32, (qrows, LANES), 1)
    rel0 = (lax.broadcasted_iota(jnp.int32, (qrows, MOBA_BLOCK), 0)
            - lax.broadcasted_iota(jnp.int32, (qrows, MOBA_BLOCK), 1))
    causal = rel0 >= 0
    e_rows = lax.broadcasted_iota(jnp.int32, (LANES, MOBA_BLOCK), 0)
    nb_pad = -(-nb // SUBLANES) * SUBLANES
    assert nb_pad <= LANES
    blk_t = lax.broadcasted_iota(jnp.int32, (nb_pad, qrows), 0)
    hmask = [lane_q // MOBA_DH == x for x in range(heads)]
    slopes = [slopes_ref[hp * heads + x] for x in range(heads)]
    srel = [slopes[x] * rel0.astype(F32) for x in range(heads)]
    kmean = jnp.concatenate(
        [jnp.mean(k_ref[0, j * MOBA_BLOCK:(j + 1) * MOBA_BLOCK, :], axis=0, keepdims=True)
         for j in range(nb)] + [jnp.zeros((LANES - nb, LANES), F32)], axis=0)

    def q_block(cur, carry):
        r0 = pl.multiple_of(cur * qrows, qrows)
        q = q_ref[0, pl.ds(r0, qrows), :]
        past_t = blk_t < cur
        qx, sel = [], []
        for x in range(heads):
            qh = jnp.where(hmask[x], q, 0.0)
            gate_t = lax.dot_general(kmean, qh, _NT, preferred_element_type=F32,
                                     precision=lax.Precision.HIGHEST)[:nb_pad, :]
            keep_t = past_t & _topk_mask_t(jnp.where(past_t, gate_t, NEG), nb)
            sel_t = jnp.concatenate([jnp.where(keep_t, 1.0, 0.0), jnp.zeros((LANES - nb_pad, qrows), F32)], axis=0)
            sel.append(sel_t.T.astype(BF16))
            qx.append(qh.astype(BF16))

        kb = k_ref[0, pl.ds(r0, qrows), :].astype(BF16)
        vb = v_ref[0, pl.ds(r0, qrows), :].astype(BF16)
        init = []
        for x in range(heads):
            s = lax.dot_general(qx[x], kb, _NT, preferred_element_type=F32) * scale - srel[x]
            s = jnp.where(causal, s, NEG)
            m = jnp.max(s, axis=-1, keepdims=True)
            p = jnp.exp(s - m)
            init.append((m, jnp.sum(p, axis=-1, keepdims=True),
                         jnp.dot(p.astype(BF16), vb, preferred_element_type=F32)))

        def past_block(j, st):
            k0 = pl.multiple_of(j * MOBA_BLOCK, MOBA_BLOCK)
            kb = k_ref[0, pl.ds(k0, MOBA_BLOCK), :].astype(BF16)
            vb = v_ref[0, pl.ds(k0, MOBA_BLOCK), :].astype(BF16)
            ej = jnp.where(e_rows == j, 1.0, 0.0).astype(BF16)
            off = (r0 - k0).astype(F32)
            out = []
            for x in range(heads):
                m, l, acc = st[x]
                selb = jnp.dot(sel[x], ej, preferred_element_type=F32)
                s = lax.dot_general(qx[x], kb, _NT, preferred_element_type=F32) * scale
                s = s - (srel[x] + slopes[x] * off)
                s = jnp.where(selb > 0.5, s, NEG)
                m_new = jnp.maximum(m, jnp.max(s, axis=-1, keepdims=True))
                alpha = jnp.exp(m - m_new)
                p = jnp.exp(s - m_new)
                acc = alpha * acc + jnp.dot(p.astype(BF16), vb, preferred_element_type=F32)
                out.append((m_new, alpha * l + jnp.sum(p, axis=-1, keepdims=True), acc))
            return tuple(out)

        st = lax.fori_loop(0, cur, past_block, tuple(init))
        o = st[heads - 1][2] / st[heads - 1][1]
        for x in range(heads - 2, -1, -1):
            o = jnp.where(hmask[x], st[x][2] / st[x][1], o)
        o_ref[0, pl.ds(r0, qrows), :] = o.astype(o_ref.dtype)
        return carry

    lax.fori_loop(0, nb, q_block, 0)


def _moba_prompt(qb, kb, vb):
    bsz, t, _ = qb.shape
    blk = pl.BlockSpec((1, t, LANES), lambda b, h: (b, 0, h))
    return pl.pallas_call(
        functools.partial(_moba_prompt_kernel, seq=t),
        grid=(bsz, MOBA_W // LANES),
        in_specs=[pl.BlockSpec(memory_space=pltpu.SMEM), blk, blk, blk],
        out_specs=blk,
        out_shape=jax.ShapeDtypeStruct((bsz, t, MOBA_W), BF16),
        compiler_params=_cparams(("parallel", "parallel")),
        name="moba_prompt",
    )(_alibi_slopes(), qb, kb, vb)


def _moba_sample_kernel(pt_ref, slopes_ref, q_ref, kn_ref, vn_ref, *rest, n_new, n_past_blocks, past_len,
                        blocks_per_step, pages_per_block):
    del pt_ref
    n_pg = blocks_per_step * pages_per_block
    k_refs, v_refs = rest[:n_pg], rest[n_pg:2 * n_pg]
    o_ref, qbd, gate_s, m_s, l_s, o_s = rest[2 * n_pg:]
    j = pl.program_id(1)
    rows = n_new * MOBA_HEADS
    page = k_refs[0].shape[2]
    scale = MOBA_DH ** -0.5
    row_i = lax.broadcasted_iota(jnp.int32, (rows, 1), 0)
    head_r = row_i % MOBA_HEADS
    tok_r = row_i // MOBA_HEADS
    slope = jnp.zeros((rows, 1), F32)
    for h in range(MOBA_HEADS):
        slope = jnp.where(head_r == h, slopes_ref[h], slope)
    head_mask = (lax.broadcasted_iota(jnp.int32, (rows, MOBA_W), 1) // MOBA_DH
                 == lax.broadcasted_iota(jnp.int32, (rows, MOBA_W), 0) % MOBA_HEADS)
    key_i = lax.broadcasted_iota(jnp.int32, (rows, page), 1)
    lane_nb = lax.broadcasted_iota(jnp.int32, (rows, LANES), 1)

    @pl.when(j == 0)
    def _():
        q = q_ref[0]
        qrep = jnp.concatenate([jnp.broadcast_to(q[t:t + 1, :], (MOBA_HEADS, MOBA_W))
                                for t in range(n_new)], axis=0)
        qbd[...] = jnp.where(head_mask, qrep, 0.0)
        gate_s[...] = jnp.full(gate_s.shape, NEG, F32)
        m_s[...] = jnp.full(m_s.shape, NEG, F32)
        l_s[...] = jnp.zeros(l_s.shape, F32)

    qf = qbd[...]
    q16 = qf.astype(BF16)
    gate_all, m_all, l_all = gate_s[...], m_s[...], l_s[...]
    kt_all = jnp.concatenate([r[0].astype(BF16) for r in k_refs], axis=1)
    raw_all = jnp.dot(q16, kt_all, preferred_element_type=F32)
    for bi in range(blocks_per_step):
        blk = j * blocks_per_step + bi
        ss = []
        gate_j = jnp.zeros((rows, 1), F32)
        for o in range(pages_per_block):
            c0 = (bi * pages_per_block + o) * page
            raw = raw_all[:, c0:c0 + page]
            gate_j = gate_j + jnp.sum(raw, axis=-1, keepdims=True)
            dist = (past_len - blk * MOBA_BLOCK - o * page) + tok_r - key_i
            ss.append(raw * scale - slope * dist.astype(F32))
        gate_j = gate_j * (1.0 / MOBA_BLOCK)
        m = jnp.max(ss[0], axis=-1, keepdims=True)
        for s in ss[1:]:
            m = jnp.maximum(m, jnp.max(s, axis=-1, keepdims=True))
        l = jnp.zeros((rows, 1), F32)
        o_blk = jnp.zeros((rows, MOBA_W), F32)
        for o, s in enumerate(ss):
            p = jnp.exp(s - m)
            l = l + jnp.sum(p, axis=-1, keepdims=True)
            vt = v_refs[bi * pages_per_block + o][0].astype(BF16)
            o_blk = o_blk + lax.dot_general(p.astype(BF16), vt, _NT, preferred_element_type=F32)
        here = lane_nb == blk
        gate_all = jnp.where(here, gate_j, gate_all)
        m_all = jnp.where(here, m, m_all)
        l_all = jnp.where(here, l, l_all)
        o_s[blk] = o_blk
    gate_s[...] = gate_all
    m_s[...] = m_all
    l_s[...] = l_all

    @pl.when(j == pl.num_programs(1) - 1)
    def _():
        sel = _topk_mask(gate_s[...], n_past_blocks) & (lane_nb < n_past_blocks)
        kn = kn_ref[0]
        vn = vn_ref[0]
        s_own = []
        for c in range(n_new):
            sc = jnp.sum(qf * kn[c:c + 1, :], axis=-1, keepdims=True) * scale
            sc = sc - slope * (tok_r - c).astype(F32)
            s_own.append(jnp.where(tok_r >= c, sc, NEG))
        m_sel = jnp.where(sel, m_s[...], NEG)
        m_tot = jnp.max(m_sel, axis=-1, keepdims=True)
        for sc in s_own:
            m_tot = jnp.maximum(m_tot, sc)
        w = jnp.where(sel, jnp.exp(m_sel - m_tot), 0.0)
        denom = jnp.sum(w * l_s[...], axis=-1, keepdims=True)
        num = jnp.zeros((rows, MOBA_W), F32)
        for c, sc in enumerate(s_own):
            pc = jnp.exp(sc - m_tot)
            denom = denom + pc
            num = num + pc * vn[c:c + 1, :]

        def add_block(jj, acc):
            wj = jnp.sum(jnp.where(lane_nb == jj, w, 0.0), axis=-1, keepdims=True)
            return acc + wj * o_s[jj]

        num = lax.fori_loop(0, n_past_blocks, add_block, num)
        res = jnp.where(head_mask, num / denom, 0.0)
        o_ref[0] = jnp.concatenate(
            [jnp.sum(res[t * MOBA_HEADS:(t + 1) * MOBA_HEADS, :], axis=0, keepdims=True)
             for t in range(n_new)], axis=0).astype(o_ref.dtype)


MOBA_SAMPLE_BLOCKS_PER_STEP = 4


def _moba_sample(qb, kb, vb, cache_k, cache_v, page_table):
    bsz, n_new, _ = qb.shape
    n_pages = page_table.shape[1]
    page = cache_k.shape[1]
    ppb = MOBA_BLOCK // page
    assert ppb * page == MOBA_BLOCK and n_pages % ppb == 0
    nbp = n_pages // ppb
    bps = MOBA_SAMPLE_BLOCKS_PER_STEP if nbp % MOBA_SAMPLE_BLOCKS_PER_STEP == 0 else 1
    assert nbp <= LANES
    rows = n_new * MOBA_HEADS
    to_pages = lambda c: jnp.transpose(c, (0, 2, 3, 1)).reshape(c.shape[0], MOBA_W, page)
    new = pl.BlockSpec((1, n_new, MOBA_W), lambda b, j, pt: (b, 0, 0))
    n_pg = bps * ppb
    pg = lambda o: pl.BlockSpec((1, MOBA_W, page), lambda b, j, pt: (pt[b, n_pg * j + o], 0, 0))
    grid_spec = pltpu.PrefetchScalarGridSpec(
        num_scalar_prefetch=1,
        grid=(bsz, nbp // bps),
        in_specs=[pl.BlockSpec(memory_space=pltpu.SMEM), new, new, new]
                 + [pg(o) for o in range(n_pg)] + [pg(o) for o in range(n_pg)],
        out_specs=new,
        scratch_shapes=[pltpu.VMEM((rows, MOBA_W), F32), pltpu.VMEM((rows, LANES), F32),
                        pltpu.VMEM((rows, LANES), F32), pltpu.VMEM((rows, LANES), F32),
                        pltpu.VMEM((nbp, rows, MOBA_W), F32)],
    )
    ck, cv = to_pages(cache_k), to_pages(cache_v)
    return pl.pallas_call(
        functools.partial(_moba_sample_kernel, n_new=n_new, n_past_blocks=nbp, past_len=n_pages * page,
                          blocks_per_step=bps, pages_per_block=ppb),
        grid_spec=grid_spec,
        out_shape=jax.ShapeDtypeStruct((bsz, n_new, MOBA_W), BF16),
        compiler_params=_cparams(("parallel", "arbitrary")),
        name="moba_sample",
    )(page_table, _alibi_slopes(), qb, kb, vb, *([ck] * n_pg), *([cv] * n_pg))


def _merge_kernel(x_ref, oa_ref, ob_ref, ga_ref, gb_ref, wpa, wpb, wout, n2, wq, keys,
                  x1_ref, xnt_ref, st_ref):
    ya = jnp.dot(oa_ref[...], wpa[...], preferred_element_type=F32)
    yb = jnp.dot(ob_ref[...], wpb[...], preferred_element_type=F32)
    merged = jax.nn.sigmoid(ga_ref[...]) * ya + jax.nn.sigmoid(gb_ref[...]) * yb
    x1 = x_ref[...] + jnp.dot(merged.astype(BF16), wout[...], preferred_element_type=F32)
    x1_ref[...] = x1
    xn = _rms(x1, n2[...])
    xnt_ref[...] = xn.T.astype(BF16)
    q = jnp.dot(xn.astype(BF16), wq[...], preferred_element_type=F32).astype(BF16)
    for i in range(2 * PEER_HEADS):
        st_ref[i * PEER_NKEYS:(i + 1) * PEER_NKEYS, :] = lax.dot_general(
            keys[i], q[:, i * PEER_HALF:(i + 1) * PEER_HALF], _NT, preferred_element_type=F32)


def _merge(x2, oa, ob, ga, gb, w_pa, w_pb, w_out, n2, wq, keys, tm):
    n = x2.shape[0]
    ws = [w_pa.astype(BF16), w_pb.astype(BF16), w_out.astype(BF16), n2.reshape(1, -1), wq.astype(BF16),
          keys.astype(BF16).reshape(2 * PEER_HEADS, PEER_NKEYS, PEER_HALF)]
    row = lambda w: pl.BlockSpec((tm, w), lambda i: (i, 0))
    full = lambda a: pl.BlockSpec(a.shape, lambda i: (0,) * a.ndim)
    n_s = 2 * PEER_HEADS * PEER_NKEYS
    return pl.pallas_call(
        _merge_kernel,
        grid=(n // tm,),
        in_specs=[row(D_MODEL), row(GLA_V_W), row(MOBA_W), row(D_MODEL), row(D_MODEL)] + [full(w) for w in ws],
        out_specs=[row(D_MODEL), pl.BlockSpec((D_MODEL, tm), lambda i: (0, i)),
                   pl.BlockSpec((n_s, tm), lambda i: (0, i))],
        out_shape=[jax.ShapeDtypeStruct((n, D_MODEL), F32), jax.ShapeDtypeStruct((D_MODEL, n), BF16),
                   jax.ShapeDtypeStruct((n_s, n), F32)],
        compiler_params=_cparams(("parallel",)),
        name="merge",
    )(x2, oa, ob, ga, gb, *ws)


def _extract_top(vals, n):
    out = []
    for i in range(n):
        m = jnp.max(vals, axis=0, keepdims=True)
        out.append(m)
        if i + 1 < n:
            vals = jnp.where(vals == m, -jnp.inf, vals)
    return out


def _peer_select_kernel(st_ref, e1_ref, e2_ref, tau_ref, *, lane_tiles):
    def head(idx, carry):
        h = idx // lane_tiles
        ls = pl.ds(pl.multiple_of((idx % lane_tiles) * LANES, LANES), LANES)
        r1 = pl.multiple_of(h * 2 * PEER_NKEYS, 2 * PEER_NKEYS)
        r2 = pl.multiple_of(h * 2 * PEER_NKEYS + PEER_NKEYS, PEER_NKEYS)
        ro = pl.multiple_of(h * PEER_NKEYS, PEER_NKEYS)
        s1 = st_ref[pl.ds(r1, PEER_NKEYS), ls]
        s2 = st_ref[pl.ds(r2, PEER_NKEYS), ls]
        v1 = _extract_top(s1, PEER_TOPK)
        v2 = _extract_top(s2, PEER_TOPK)
        v1a = jnp.concatenate(v1, axis=0)
        v2a = jnp.concatenate(v2, axis=0)
        cands = [v1[0] + v2a, v1[1] + v2a[:8], v1[2] + v2a[:8], v1[3] + v2a[:8]]
        cands += [v1[i] + v2a[:8] for i in range(4, 8)]
        cands.append(v1a[8:] + v2[0])
        cand = jnp.concatenate(cands, axis=0)
        best = _extract_top(cand, PEER_TOPK)
        z = jnp.ones_like(best[0])
        for bk in best[1:]:
            z = z + jnp.exp(bk - best[0])
        e1_ref[pl.ds(ro, PEER_NKEYS), ls] = jnp.exp(s1 - v1[0]) / z
        e2_ref[pl.ds(ro, PEER_NKEYS), ls] = jnp.exp(s2 - v2[0])
        tau_ref[pl.ds(pl.multiple_of(h * SUBLANES, SUBLANES), SUBLANES), ls] = jnp.broadcast_to(
            best[-1], (SUBLANES, LANES))
        return carry

    n_units = PEER_HEADS * lane_tiles
    per_it = 4 if n_units % 4 == 0 else 1

    def units(i, carry):
        for u in range(per_it):
            head(i * per_it + u, carry)
        return carry

    lax.fori_loop(0, n_units // per_it, units, 0)


def _peer_select(st, tn):
    n_s, n = st.shape
    n_e = PEER_HEADS * PEER_NKEYS
    return pl.pallas_call(
        functools.partial(_peer_select_kernel, lane_tiles=tn // LANES),
        grid=(n // tn,),
        in_specs=[pl.BlockSpec((n_s, tn), lambda i: (0, i))],
        out_specs=[pl.BlockSpec((n_e, tn), lambda i: (0, i)), pl.BlockSpec((n_e, tn), lambda i: (0, i)),
                   pl.BlockSpec((PEER_HEADS * SUBLANES, tn), lambda i: (0, i))],
        out_shape=[jax.ShapeDtypeStruct((n_e, n), F32), jax.ShapeDtypeStruct((n_e, n), F32),
                   jax.ShapeDtypeStruct((PEER_HEADS * SUBLANES, n), F32)],
        compiler_params=_cparams(("parallel",)),
        name="peer_select",
    )(st)


def _zero_from(x):
    u = lax.bitcast_convert_type(x, jnp.uint32)
    u = lax.shift_right_logical(lax.shift_right_logical(u, jnp.uint32(16)), jnp.uint32(16))
    return lax.bitcast_convert_type(u, F32)


def _peer_dense_kernel(xnt_ref, x1_ref, st_ref, e1_ref, e2_ref, tau_ref, u_ref, vt_ref, fg_ref, y_ref,
                       acc, hbuf0, hbuf1, act0, act1, *, eb, tn, n_blocks):
    jj = pl.program_id(1)
    hbuf, act = (hbuf0, hbuf1), (act0, act1)
    groups = eb // PEER_NKEYS
    assert groups == SUBLANES, "one aligned 8-row load of first-key scores per expert block"
    inv_sqrt2 = 1.0 / math.sqrt(2.0)

    tiles_per_it = math.gcd(PEER_LANE_TILES_PER_ITER, tn // LANES)
    n_it = tn // (tiles_per_it * LANES)
    h_rows = eb // n_it
    o_rows = D_MODEL // n_it

    n_strips_tile = PEER_NKEYS // PEER_KEY_STRIP

    def tile_rows(ls):
        a0 = (jj - 1) * SUBLANES
        s1b = [st_ref[pl.ds(pl.multiple_of(h * 2 * PEER_NKEYS + a0, SUBLANES), SUBLANES), ls]
               for h in range(PEER_HEADS)]
        e1b = [e1_ref[pl.ds(pl.multiple_of(h * PEER_NKEYS + a0, SUBLANES), SUBLANES), ls]
               for h in range(PEER_HEADS)]
        taus = [tau_ref[h * SUBLANES:h * SUBLANES + 1, ls] for h in range(PEER_HEADS)]
        return s1b, e1b, taus

    def mixture_strip(slot, ls, rows, si, pin):
        s1b, e1b, taus = rows
        b0 = si * PEER_KEY_STRIP
        w = [jnp.zeros((PEER_KEY_STRIP, LANES), F32) for _ in range(groups)]
        for h in range(PEER_HEADS):
            r2 = h * 2 * PEER_NKEYS + PEER_NKEYS + b0
            s2 = st_ref[r2:r2 + PEER_KEY_STRIP, ls]
            e2 = e2_ref[h * PEER_NKEYS + b0:h * PEER_NKEYS + b0 + PEER_KEY_STRIP, ls]
            for ai in range(groups):
                keep = s1b[h][ai:ai + 1, :] + s2 >= taus[h]
                w[ai] = w[ai] + jnp.where(keep, e2 * e1b[h][ai:ai + 1, :], 0.0)
        for ai in range(groups):
            es = slice(ai * PEER_NKEYS + b0, ai * PEER_NKEYS + b0 + PEER_KEY_STRIP)
            h_t = hbuf[slot][es, ls]
            gelu = 0.5 * h_t * (1.0 + lax.erf(h_t * inv_sqrt2))
            val = w[ai] * gelu
            if ai == groups - 1:
                val = val + jnp.concatenate([pin] * (PEER_KEY_STRIP // SUBLANES), axis=0)
            act[slot][es, ls] = val.astype(BF16)

    def sweep(slot, do_hidden, do_mixture, do_output):
        n_strips = tiles_per_it * n_strips_tile
        half = n_strips // 2
        ch_h, ch_o = h_rows // half, o_rows // half

        def step(it, carry):
            h0 = pl.multiple_of(it * h_rows, h_rows)
            o0 = pl.multiple_of(it * o_rows, o_rows)
            r_h = r_o = None
            if do_hidden:
                r_h = jnp.dot(u_ref[pl.ds(h0, h_rows), :], xnt_ref[...], preferred_element_type=F32)
            if do_output:
                r_o = jnp.dot(vt_ref[pl.ds(o0, o_rows), :], act[1 - slot][...], preferred_element_type=F32)
            if not do_mixture:
                if do_hidden:
                    hbuf[1 - slot][pl.ds(h0, h_rows), :] = r_h
                if do_output:
                    acc[pl.ds(o0, o_rows), :] += r_o
                return carry
            for idx in range(n_strips):
                k, si = divmod(idx, n_strips_tile)
                ls = pl.ds(pl.multiple_of((it * tiles_per_it + k) * LANES, LANES), LANES)
                if si == 0:
                    rows = tile_rows(ls)
                if idx < half:
                    c = idx * ch_h
                    mixture_strip(slot, ls, rows, si, _zero_from(r_h[c:c + SUBLANES, 0:LANES]))
                    hbuf[1 - slot][pl.ds(pl.multiple_of(h0 + c, ch_h), ch_h), :] = r_h[c:c + ch_h, :]
                else:
                    c = (idx - half) * ch_o
                    mixture_strip(slot, ls, rows, si, _zero_from(r_o[c:c + SUBLANES, 0:LANES]))
                    acc[pl.ds(pl.multiple_of(o0 + c, ch_o), ch_o), :] += r_o[c:c + ch_o, :]
            return carry
        lax.fori_loop(0, n_it, step, 0)

    @pl.when(jj == 0)
    def _():
        acc[...] = jnp.zeros(acc.shape, F32)
        act[0][...] = jnp.zeros(act[0].shape, BF16)
        sweep(0, True, False, False)

    for slot in range(2):
        @pl.when((jj >= 1) & (jj <= n_blocks) & (jj % 2 == slot))
        def _(slot=slot):
            sweep(slot, True, True, True)

    @pl.when(jj == n_blocks + 1)
    def _():
        sweep((n_blocks + 1) % 2, False, False, True)
        y = x1_ref[...] + acc[...].T
        y_ref[...] = _rms(y, fg_ref[...])


def _peer_dense(xnt, x1, st, e1, e2, tau, u_bf, vt_bf, final_g, tn, eb):
    n = x1.shape[0]
    n_blocks = PEER_N // eb
    tok = lambda w: pl.BlockSpec((tn, w), lambda i, j: (i, 0))
    col = lambda a: pl.BlockSpec((a.shape[0], tn), lambda i, j: (0, i))
    return pl.pallas_call(
        functools.partial(_peer_dense_kernel, eb=eb, tn=tn, n_blocks=n_blocks),
        grid=(n // tn, n_blocks + 2),
        in_specs=[col(xnt), tok(D_MODEL), col(st), col(e1), col(e2), col(tau),
                  pl.BlockSpec((eb, D_MODEL), lambda i, j: (jnp.minimum(j, n_blocks - 1), 0)),
                  pl.BlockSpec((D_MODEL, eb), lambda i, j: (0, jnp.clip(j - 2, 0, n_blocks - 1))),
                  pl.BlockSpec((1, D_MODEL), lambda i, j: (0, 0))],
        out_specs=tok(D_MODEL),
        out_shape=jax.ShapeDtypeStruct((n, D_MODEL), F32),
        scratch_shapes=[pltpu.VMEM((D_MODEL, tn), F32), pltpu.VMEM((eb, tn), F32), pltpu.VMEM((eb, tn), F32),
                        pltpu.VMEM((eb, tn), BF16), pltpu.VMEM((eb, tn), BF16)],
        compiler_params=_cparams(("parallel", "arbitrary")),
        name="peer_dense",
    )(xnt, x1, st, e1, e2, tau, u_bf, vt_bf, final_g.reshape(1, -1))


def _group(x, s0, moba_fn, wts, tm, tn, eb):
    (n1, w_in, wa2, ba, gla_g, w_pa, w_pb, w_out, n2, wq, keys, u_bf, vt_bf, final_g) = wts
    bsz, t, _ = x.shape
    n = bsz * t
    x2 = x.reshape(n, D_MODEL)
    qa, ka, va, ra, la, qb, kb, vb, ga, gb = _in_proj(x2, n1, w_in, wa2, ba, tm)
    r3 = lambda a: a.reshape(bsz, t, a.shape[-1])
    oa, s_fin = _gla(r3(qa), r3(ka), r3(va), r3(la), r3(ra), s0, gla_g)
    ob = moba_fn(r3(qb), r3(kb), r3(vb))
    x1, xn, st = _merge(x2, oa.reshape(n, GLA_V_W), ob.reshape(n, MOBA_W), ga, gb,
                        w_pa, w_pb, w_out, n2, wq, keys, tm)
    e1, e2, tau = _peer_select(st, tn)
    y = _peer_dense(xn, x1, st, e1, e2, tau, u_bf, vt_bf, final_g, tn, eb)
    kv_shape = (1, bsz, t, MOBA_HEADS, MOBA_DH)
    return y.reshape(bsz, t, D_MODEL), kb.reshape(kv_shape), vb.reshape(kv_shape), s_fin[None]


def kernel(x_prompt, x_sample, cache_k, cache_v, state_gla, page_table, norm1_g, w_in, gla_wa2, gla_ba,
           gla_norm_g, w_pa, w_pb, w_out, norm2_g, peer_wq, peer_keys, peer_u, peer_v, final_g):
    assert w_in.shape[0] == 1, "single-layer step"
    u_bf = peer_u[0].astype(BF16)
    vt_bf = peer_v[0].astype(BF16).T
    wts = (norm1_g[0], w_in[0], gla_wa2[0], gla_ba[0], gla_norm_g[0], w_pa[0], w_pb[0], w_out[0],
           norm2_g[0], peer_wq[0], peer_keys[0], u_bf, vt_bf, final_g)
    bp = x_prompt.shape[0]
    s0p = jnp.zeros((bp, GLA_HEADS, GLA_DK, GLA_DV), F32)
    n_s = x_sample.shape[0] * x_sample.shape[1]
    tok_blk = lambda n, pref: pref if n % pref == 0 else n
    n_p = bp * x_prompt.shape[1]
    yp, kp, vp, sp = _group(x_prompt, s0p, _moba_prompt, wts,
                            tok_blk(n_p, 256), tok_blk(n_p, 512), PEER_EXPERT_BLOCK)
    ck, cv = cache_k[0], cache_v[0]
    ys, ks, vs, ss = _group(x_sample, state_gla[0],
                            lambda q, k, v: _moba_sample(q, k, v, ck, cv, page_table), wts,
                            tok_blk(n_s, 256), tok_blk(n_s, 512), PEER_EXPERT_BLOCK)
    return (yp, ys, kp, vp, sp, ks, vs, ss)
```

```python
import functools
import math

import jax
import jax.numpy as jnp
from jax import lax
from jax.experimental import pallas as pl
from jax.experimental.pallas import tpu as pltpu

F32 = jnp.float32
BF16 = jnp.bfloat16

D_MODEL = 1024
GLA_HEADS = 4
GLA_DK = 64
GLA_DV = 128
GLA_RANK = 16
GLA_TAU = 16.0
GLA_CHUNK = 64
GLA_QK_W = GLA_HEADS * GLA_DK
GLA_V_W = GLA_HEADS * GLA_DV
MOBA_HEADS = 8
MOBA_DH = 64
MOBA_W = MOBA_HEADS * MOBA_DH
MOBA_BLOCK = 256
MOBA_TOPK = 3
MOBA_QCHUNK = 128
NEG = -1e30
PEER_HEADS = 8
PEER_NKEYS = 128
PEER_N = PEER_NKEYS * PEER_NKEYS
PEER_QDIM = 256
PEER_HALF = PEER_QDIM // 2
PEER_TOPK = 16
EPS = 1e-6
IN_SIZES = (GLA_QK_W, GLA_QK_W, GLA_V_W, GLA_V_W, GLA_RANK, MOBA_W, MOBA_W, MOBA_W, D_MODEL, D_MODEL)

LANES = 128
SUBLANES = 8
PEER_EXPERT_BLOCK = SUBLANES * PEER_NKEYS
PEER_KEY_STRIP = 16
PEER_LANE_TILES_PER_ITER = 2
VMEM_LIMIT = 56 * 1024 * 1024

_NT = (((1,), (1,)), ((), ()))
_TN = (((0,), (0,)), ((), ()))


def _cparams(sem):
    return pltpu.CompilerParams(dimension_semantics=sem, vmem_limit_bytes=VMEM_LIMIT)


def _rms(x, g):
    return x * lax.rsqrt(jnp.mean(x * x, axis=-1, keepdims=True) + EPS) * g


def _inproj_kernel(x_ref, g_ref, wqa, wka, wva, wra, wlr, wa2, ba, wqb, wkb, wvb, wga, wgb,
                   qa, ka, va, ra, la, qb, kb, vb, ga, gb):
    xn = _rms(x_ref[...], g_ref[...]).astype(BF16)
    for w, o in ((wqa, qa), (wka, ka), (wva, va), (wra, ra), (wqb, qb), (wkb, kb), (wvb, vb),
                 (wga, ga), (wgb, gb)):
        o[...] = jnp.dot(xn, w[...], preferred_element_type=F32).astype(o.dtype)
    lr = jnp.dot(xn, wlr[...], preferred_element_type=F32).astype(BF16)
    z = jnp.dot(lr, wa2[...], preferred_element_type=F32) + ba[...]
    la[...] = (jnp.minimum(z, 0.0) - jnp.log1p(jnp.exp(-jnp.abs(z)))) * (1.0 / GLA_TAU)


def _in_proj(x2, n1, w_in, wa2, ba, tm):
    n = x2.shape[0]
    offs = [0]
    for c in IN_SIZES:
        offs.append(offs[-1] + c)
    wb = w_in.astype(BF16)
    piece = lambda i: wb[:, offs[i]:offs[i + 1]]
    wlr = jnp.pad(piece(4), ((0, 0), (0, LANES - GLA_RANK)))
    wa2p = jnp.pad(wa2.astype(BF16), ((0, LANES - GLA_RANK), (0, 0)))
    weights = [piece(0), piece(1), piece(2), piece(3), wlr, wa2p, ba.reshape(1, -1),
               piece(5), piece(6), piece(7), piece(8), piece(9)]
    widths = [GLA_QK_W, GLA_QK_W, GLA_V_W, GLA_V_W, GLA_QK_W, MOBA_W, MOBA_W, MOBA_W, D_MODEL, D_MODEL]
    row = lambda w: pl.BlockSpec((tm, w), lambda i: (i, 0))
    full = lambda a: pl.BlockSpec(a.shape, lambda i: (0, 0))
    return pl.pallas_call(
        _inproj_kernel,
        grid=(n // tm,),
        in_specs=[row(D_MODEL), full(n1.reshape(1, -1))] + [full(w) for w in weights],
        out_specs=[row(w) for w in widths],
        out_shape=[jax.ShapeDtypeStruct((n, w), F32) for w in widths],
        compiler_params=_cparams(("parallel",)),
        name="in_proj",
    )(x2, n1.reshape(1, -1), *weights)


def _gla_kernel(qa, ka, va, la, ra, s0, g_ref, oa, sfin, s_scr, *, chunk, n_chunks, seqs):
    t = pl.program_id(1)

    @pl.when(t == 0)
    def _():
        s_scr[...] = s0[...]

    ri = lax.broadcasted_iota(jnp.int32, (chunk, chunk), 0)
    ci = lax.broadcasted_iota(jnp.int32, (chunk, chunk), 1)
    causal = ci <= ri
    tril = causal.astype(F32)
    eye_k = (lax.broadcasted_iota(jnp.int32, (GLA_DK, GLA_DK), 0)
             == lax.broadcasted_iota(jnp.int32, (GLA_DK, GLA_DK), 1))
    ones_kv = jnp.ones((GLA_DK, GLA_DV), F32)
    g = g_ref[...]

    def body(c, carry):
        r0 = pl.multiple_of(c * chunk, chunk)
        rows = pl.ds(r0, chunk)
        for bi in range(seqs):
            a = la[bi, rows, :]
            b = jnp.dot(tril, a, preferred_element_type=F32, precision=lax.Precision.HIGHEST)
            b_last = b[chunk - 1:chunk, :]
            q = qa[bi, rows, :] * (GLA_DK ** -0.5)
            k = ka[bi, rows, :]
            q_dec = (q * jnp.exp(b)).astype(BF16)
            k_inv = (k * jnp.exp(-b)).astype(BF16)
            k_up = (k * jnp.exp(b_last - b)).astype(BF16)
            dec = jnp.exp(b_last)
            outs = []
            for h in range(GLA_HEADS):
                ks = slice(h * GLA_DK, (h + 1) * GLA_DK)
                vs = slice(h * GLA_DV, (h + 1) * GLA_DV)
                s = s_scr[bi, h]
                v = va[bi, rows, vs].astype(BF16)
                o = jnp.dot(q_dec[:, ks], s.astype(BF16), preferred_element_type=F32)
                att = lax.dot_general(q_dec[:, ks], k_inv[:, ks], _NT, preferred_element_type=F32)
                att = jnp.where(causal, att, 0.0).astype(BF16)
                o = o + jnp.dot(att, v, preferred_element_type=F32)
                kv = lax.dot_general(k_up[:, ks], v, _TN, preferred_element_type=F32)
                dmat = jnp.where(eye_k, jnp.broadcast_to(dec[:, ks], (GLA_DK, GLA_DK)), 0.0)
                dcol = jnp.dot(dmat, ones_kv, preferred_element_type=F32, precision=lax.Precision.HIGHEST)
                s_scr[bi, h] = dcol * s + kv
                on = _rms(o, g)
                r = ra[bi, rows, vs]
                outs.append(on * (r * jax.nn.sigmoid(r)))
            oa[bi, rows, :] = jnp.concatenate(outs, axis=-1).astype(oa.dtype)
        return carry

    lax.fori_loop(0, n_chunks, body, 0)

    @pl.when(t == pl.num_programs(1) - 1)
    def _():
        sfin[...] = s_scr[...]


GLA_MIN_CHUNK = 16
GLA_SEQS_PER_STEP = 2


def _gla(qa, ka, va, la, ra, s0, gla_g):
    bsz, t_real, _ = qa.shape
    chunk = math.gcd(GLA_CHUNK, t_real)
    if chunk < GLA_MIN_CHUNK:
        chunk = GLA_MIN_CHUNK
        pad = (-t_real) % chunk
        qa, ka, va, la, ra = (jnp.pad(a, ((0, 0), (0, pad), (0, 0))) for a in (qa, ka, va, la, ra))
    t = qa.shape[1]
    tb = min(t, 512)
    seqs = math.gcd(GLA_SEQS_PER_STEP, bsz)
    blk = lambda w: pl.BlockSpec((seqs, tb, w), lambda b, i: (b, i, 0))
    st = pl.BlockSpec((seqs, GLA_HEADS, GLA_DK, GLA_DV), lambda b, i: (b, 0, 0, 0))
    oa, s_fin = pl.pallas_call(
        functools.partial(_gla_kernel, chunk=chunk, n_chunks=tb // chunk, seqs=seqs),
        grid=(bsz // seqs, t // tb),
        in_specs=[blk(GLA_QK_W), blk(GLA_QK_W), blk(GLA_V_W), blk(GLA_QK_W), blk(GLA_V_W), st,
                  pl.BlockSpec((1, GLA_DV), lambda b, i: (0, 0))],
        out_specs=[blk(GLA_V_W), st],
        out_shape=[jax.ShapeDtypeStruct((bsz, t, GLA_V_W), BF16),
                   jax.ShapeDtypeStruct((bsz, GLA_HEADS, GLA_DK, GLA_DV), F32)],
        scratch_shapes=[pltpu.VMEM((seqs, GLA_HEADS, GLA_DK, GLA_DV), F32)],
        compiler_params=_cparams(("parallel", "arbitrary")),
        name="gla",
    )(qa, ka, va, la, ra, s0, gla_g.reshape(1, -1))
    return oa[:, :t_real], s_fin


def _alibi_slopes():
    return jnp.exp2(-8.0 * jnp.arange(1, MOBA_HEADS + 1, dtype=F32) / MOBA_HEADS)


def _topk_mask(gm, n_blocks):
    lane = lax.broadcasted_iota(jnp.int32, gm.shape, 1)
    rank = jnp.zeros(gm.shape, F32)
    for j in range(n_blocks):
        col = gm[:, j:j + 1]
        ahead = (col > gm) | ((col == gm) & (j < lane))
        rank = rank + jnp.where(ahead, 1.0, 0.0)
    return rank < float(MOBA_TOPK)


def _topk_mask_t(gm, n_blocks):
    row = lax.broadcasted_iota(jnp.int32, gm.shape, 0)
    rank = jnp.zeros(gm.shape, F32)
    for j in range(n_blocks):
        other = gm[j:j + 1, :]
        ahead = (other > gm) | ((other == gm) & (j < row))
        rank = rank + jnp.where(ahead, 1.0, 0.0)
    return rank < float(MOBA_TOPK)


def _moba_prompt_kernel(slopes_ref, q_ref, k_ref, v_ref, o_ref, *, seq):
    nb = seq // MOBA_BLOCK
    qrows = MOBA_BLOCK
    heads = LANES // MOBA_DH
    scale = MOBA_DH ** -0.5
    hp = pl.program_id(1)
    lane_q = lax.broadcasted_iota(jnp.int32, (qrows, LANES), 1)
    rel0 = (lax.broadcasted_iota(jnp.int32, (qrows, MOBA_BLOCK), 0)
            - lax.broadcasted_iota(jnp.int32, (qrows, MOBA_BLOCK), 1))
    causal = rel0 >= 0
    e_rows = lax.broadcasted_iota(jnp.int32, (LANES, MOBA_BLOCK), 0)
    nb_pad = -(-nb // SUBLANES) * SUBLANES
    assert nb_pad <= LANES
    blk_t = lax.broadcasted_iota(jnp.int32, (nb_pad, qrows), 0)
    hmask = [lane_q // MOBA_DH == x for x in range(heads)]
    slopes = [slopes_ref[hp * heads + x] for x in range(heads)]
    srel = [slopes[x] * rel0.astype(F32) for x in range(heads)]
    kmean = jnp.concatenate(
        [jnp.mean(k_ref[0, j * MOBA_BLOCK:(j + 1) * MOBA_BLOCK, :], axis=0, keepdims=True)
         for j in range(nb)] + [jnp.zeros((LANES - nb, LANES), F32)], axis=0)

    def q_block(cur, carry):
        r0 = pl.multiple_of(cur * qrows, qrows)
        q = q_ref[0, pl.ds(r0, qrows), :]
        past_t = blk_t < cur
        qx, sel = [], []
        for x in range(heads):
            qh = jnp.where(hmask[x], q, 0.0)
            gate_t = lax.dot_general(kmean, qh, _NT, preferred_element_type=F32,
                                     precision=lax.Precision.HIGHEST)[:nb_pad, :]
            keep_t = past_t & _topk_mask_t(jnp.where(past_t, gate_t, NEG), nb)
            sel_t = jnp.concatenate([jnp.where(keep_t, 1.0, 0.0), jnp.zeros((LANES - nb_pad, qrows), F32)], axis=0)
            sel.append(sel_t.T.astype(BF16))
            qx.append(qh.astype(BF16))

        kb = k_ref[0, pl.ds(r0, qrows), :].astype(BF16)
        vb = v_ref[0, pl.ds(r0, qrows), :].astype(BF16)
        init = []
        for x in range(heads):
            s = lax.dot_general(qx[x], kb, _NT, preferred_element_type=F32) * scale - srel[x]
            s = jnp.where(causal, s, NEG)
            m = jnp.max(s, axis=-1, keepdims=True)
            p = jnp.exp(s - m)
            init.append((m, jnp.sum(p, axis=-1, keepdims=True),
                         jnp.dot(p.astype(BF16), vb, preferred_element_type=F32)))

        def past_block(j, st):
            k0 = pl.multiple_of(j * MOBA_BLOCK, MOBA_BLOCK)
            kb = k_ref[0, pl.ds(k0, MOBA_BLOCK), :].astype(BF16)
            vb = v_ref[0, pl.ds(k0, MOBA_BLOCK), :].astype(BF16)
            ej = jnp.where(e_rows == j, 1.0, 0.0).astype(BF16)
            off = (r0 - k0).astype(F32)
            out = []
            for x in range(heads):
                m, l, acc = st[x]
                selb = jnp.dot(sel[x], ej, preferred_element_type=F32)
                s = lax.dot_general(qx[x], kb, _NT, preferred_element_type=F32) * scale
                s = s - (srel[x] + slopes[x] * off)
                s = jnp.where(selb > 0.5, s, NEG)
                m_new = jnp.maximum(m, jnp.max(s, axis=-1, keepdims=True))
                alpha = jnp.exp(m - m_new)
                p = jnp.exp(s - m_new)
                acc = alpha * acc + jnp.dot(p.astype(BF16), vb, preferred_element_type=F32)
                out.append((m_new, alpha * l + jnp.sum(p, axis=-1, keepdims=True), acc))
            return tuple(out)

        st = lax.fori_loop(0, cur, past_block, tuple(init))
        o = st[heads - 1][2] / st[heads - 1][1]
        for x in range(heads - 2, -1, -1):
            o = jnp.where(hmask[x], st[x][2] / st[x][1], o)
        o_ref[0, pl.ds(r0, qrows), :] = o.astype(o_ref.dtype)
        return carry

    lax.fori_loop(0, nb, q_block, 0)


def _moba_prompt(qb, kb, vb):
    bsz, t, _ = qb.shape
    blk = pl.BlockSpec((1, t, LANES), lambda b, h: (b, 0, h))
    return pl.pallas_call(
        functools.partial(_moba_prompt_kernel, seq=t),
        grid=(bsz, MOBA_W // LANES),
        in_specs=[pl.BlockSpec(memory_space=pltpu.SMEM), blk, blk, blk],
        out_specs=blk,
        out_shape=jax.ShapeDtypeStruct((bsz, t, MOBA_W), BF16),
        compiler_params=_cparams(("parallel", "parallel")),
        name="moba_prompt",
    )(_alibi_slopes(), qb, kb, vb)


def _moba_sample_kernel(pt_ref, slopes_ref, q_ref, kn_ref, vn_ref, *rest, n_new, n_past_blocks, past_len,
                        blocks_per_step, pages_per_block):
    del pt_ref
    n_pg = blocks_per_step * pages_per_block
    k_refs, v_refs = rest[:n_pg], rest[n_pg:2 * n_pg]
    o_ref, qbd, gate_s, m_s, l_s, o_s = rest[2 * n_pg:]
    j = pl.program_id(1)
    rows = n_new * MOBA_HEADS
    page = k_refs[0].shape[2]
    scale = MOBA_DH ** -0.5
    row_i = lax.broadcasted_iota(jnp.int32, (rows, 1), 0)
    head_r = row_i % MOBA_HEADS
    tok_r = row_i // MOBA_HEADS
    slope = jnp.zeros((rows, 1), F32)
    for h in range(MOBA_HEADS):
        slope = jnp.where(head_r == h, slopes_ref[h], slope)
    head_mask = (lax.broadcasted_iota(jnp.int32, (rows, MOBA_W), 1) // MOBA_DH
                 == lax.broadcasted_iota(jnp.int32, (rows, MOBA_W), 0) % MOBA_HEADS)
    key_i = lax.broadcasted_iota(jnp.int32, (rows, page), 1)
    lane_nb = lax.broadcasted_iota(jnp.int32, (rows, LANES), 1)

    @pl.when(j == 0)
    def _():
        q = q_ref[0]
        qrep = jnp.concatenate([jnp.broadcast_to(q[t:t + 1, :], (MOBA_HEADS, MOBA_W))
                                for t in range(n_new)], axis=0)
        qbd[...] = jnp.where(head_mask, qrep, 0.0)
        gate_s[...] = jnp.full(gate_s.shape, NEG, F32)
        m_s[...] = jnp.full(m_s.shape, NEG, F32)
        l_s[...] = jnp.zeros(l_s.shape, F32)

    qf = qbd[...]
    q16 = qf.astype(BF16)
    gate_all, m_all, l_all = gate_s[...], m_s[...], l_s[...]
    kt_all = jnp.concatenate([r[0].astype(BF16) for r in k_refs], axis=1)
    raw_all = jnp.dot(q16, kt_all, preferred_element_type=F32)
    for bi in range(blocks_per_step):
        blk = j * blocks_per_step + bi
        ss = []
        gate_j = jnp.zeros((rows, 1), F32)
        for o in range(pages_per_block):
            c0 = (bi * pages_per_block + o) * page
            raw = raw_all[:, c0:c0 + page]
            gate_j = gate_j + jnp.sum(raw, axis=-1, keepdims=True)
            dist = (past_len - blk * MOBA_BLOCK - o * page) + tok_r - key_i
            ss.append(raw * scale - slope * dist.astype(F32))
        gate_j = gate_j * (1.0 / MOBA_BLOCK)
        m = jnp.max(ss[0], axis=-1, keepdims=True)
        for s in ss[1:]:
            m = jnp.maximum(m, jnp.max(s, axis=-1, keepdims=True))
        l = jnp.zeros((rows, 1), F32)
        o_blk = jnp.zeros((rows, MOBA_W), F32)
        for o, s in enumerate(ss):
            p = jnp.exp(s - m)
            l = l + jnp.sum(p, axis=-1, keepdims=True)
            vt = v_refs[bi * pages_per_block + o][0].astype(BF16)
            o_blk = o_blk + lax.dot_general(p.astype(BF16), vt, _NT, preferred_element_type=F32)
        here = lane_nb == blk
        gate_all = jnp.where(here, gate_j, gate_all)
        m_all = jnp.where(here, m, m_all)
        l_all = jnp.where(here, l, l_all)
        o_s[blk] = o_blk
    gate_s[...] = gate_all
    m_s[...] = m_all
    l_s[...] = l_all

    @pl.when(j == pl.num_programs(1) - 1)
    def _():
        sel = _topk_mask(gate_s[...], n_past_blocks) & (lane_nb < n_past_blocks)
        kn = kn_ref[0]
        vn = vn_ref[0]
        s_own = []
        for c in range(n_new):
            sc = jnp.sum(qf * kn[c:c + 1, :], axis=-1, keepdims=True) * scale
            sc = sc - slope * (tok_r - c).astype(F32)
            s_own.append(jnp.where(tok_r >= c, sc, NEG))
        m_sel = jnp.where(sel, m_s[...], NEG)
        m_tot = jnp.max(m_sel, axis=-1, keepdims=True)
        for sc in s_own:
            m_tot = jnp.maximum(m_tot, sc)
        w = jnp.where(sel, jnp.exp(m_sel - m_tot), 0.0)
        denom = jnp.sum(w * l_s[...], axis=-1, keepdims=True)
        num = jnp.zeros((rows, MOBA_W), F32)
        for c, sc in enumerate(s_own):
            pc = jnp.exp(sc - m_tot)
            denom = denom + pc
            num = num + pc * vn[c:c + 1, :]

        def add_block(jj, acc):
            wj = jnp.sum(jnp.where(lane_nb == jj, w, 0.0), axis=-1, keepdims=True)
            return acc + wj * o_s[jj]

        num = lax.fori_loop(0, n_past_blocks, add_block, num)
        res = jnp.where(head_mask, num / denom, 0.0)
        o_ref[0] = jnp.concatenate(
            [jnp.sum(res[t * MOBA_HEADS:(t + 1) * MOBA_HEADS, :], axis=0, keepdims=True)
             for t in range(n_new)], axis=0).astype(o_ref.dtype)


MOBA_SAMPLE_BLOCKS_PER_STEP = 8


def _moba_sample(qb, kb, vb, cache_k, cache_v, page_table):
    bsz, n_new, _ = qb.shape
    n_pages = page_table.shape[1]
    page = cache_k.shape[1]
    ppb = MOBA_BLOCK // page
    assert ppb * page == MOBA_BLOCK and n_pages % ppb == 0
    nbp = n_pages // ppb
    bps = MOBA_SAMPLE_BLOCKS_PER_STEP if nbp % MOBA_SAMPLE_BLOCKS_PER_STEP == 0 else 1
    assert nbp <= LANES
    rows = n_new * MOBA_HEADS
    to_pages = lambda c: jnp.transpose(c, (0, 2, 3, 1)).reshape(c.shape[0], MOBA_W, page)
    new = pl.BlockSpec((1, n_new, MOBA_W), lambda b, j, pt: (b, 0, 0))
    n_pg = bps * ppb
    pg = lambda o: pl.BlockSpec((1, MOBA_W, page), lambda b, j, pt: (pt[b, n_pg * j + o], 0, 0))
    grid_spec = pltpu.PrefetchScalarGridSpec(
        num_scalar_prefetch=1,
        grid=(bsz, nbp // bps),
        in_specs=[pl.BlockSpec(memory_space=pltpu.SMEM), new, new, new]
                 + [pg(o) for o in range(n_pg)] + [pg(o) for o in range(n_pg)],
        out_specs=new,
        scratch_shapes=[pltpu.VMEM((rows, MOBA_W), F32), pltpu.VMEM((rows, LANES), F32),
                        pltpu.VMEM((rows, LANES), F32), pltpu.VMEM((rows, LANES), F32),
                        pltpu.VMEM((nbp, rows, MOBA_W), F32)],
    )
    ck, cv = to_pages(cache_k), to_pages(cache_v)
    return pl.pallas_call(
        functools.partial(_moba_sample_kernel, n_new=n_new, n_past_blocks=nbp, past_len=n_pages * page,
                          blocks_per_step=bps, pages_per_block=ppb),
        grid_spec=grid_spec,
        out_shape=jax.ShapeDtypeStruct((bsz, n_new, MOBA_W), BF16),
        compiler_params=_cparams(("parallel", "arbitrary")),
        name="moba_sample",
    )(page_table, _alibi_slopes(), qb, kb, vb, *([ck] * n_pg), *([cv] * n_pg))


def _merge_kernel(x_ref, oa_ref, ob_ref, ga_ref, gb_ref, wpa, wpb, wout, n2, wq, keys,
                  x1_ref, xnt_ref, st_ref):
    ya = jnp.dot(oa_ref[...], wpa[...], preferred_element_type=F32)
    yb = jnp.dot(ob_ref[...], wpb[...], preferred_element_type=F32)
    merged = jax.nn.sigmoid(ga_ref[...]) * ya + jax.nn.sigmoid(gb_ref[...]) * yb
    x1 = x_ref[...] + jnp.dot(merged.astype(BF16), wout[...], preferred_element_type=F32)
    x1_ref[...] = x1
    xn = _rms(x1, n2[...])
    xnt_ref[...] = xn.T.astype(BF16)
    q = jnp.dot(xn.astype(BF16), wq[...], preferred_element_type=F32).astype(BF16)
    for i in range(2 * PEER_HEADS):
        st_ref[i * PEER_NKEYS:(i + 1) * PEER_NKEYS, :] = lax.dot_general(
            keys[i], q[:, i * PEER_HALF:(i + 1) * PEER_HALF], _NT, preferred_element_type=F32)


def _merge(x2, oa, ob, ga, gb, w_pa, w_pb, w_out, n2, wq, keys, tm):
    n = x2.shape[0]
    ws = [w_pa.astype(BF16), w_pb.astype(BF16), w_out.astype(BF16), n2.reshape(1, -1), wq.astype(BF16),
          keys.astype(BF16).reshape(2 * PEER_HEADS, PEER_NKEYS, PEER_HALF)]
    row = lambda w: pl.BlockSpec((tm, w), lambda i: (i, 0))
    full = lambda a: pl.BlockSpec(a.shape, lambda i: (0,) * a.ndim)
    n_s = 2 * PEER_HEADS * PEER_NKEYS
    return pl.pallas_call(
        _merge_kernel,
        grid=(n // tm,),
        in_specs=[row(D_MODEL), row(GLA_V_W), row(MOBA_W), row(D_MODEL), row(D_MODEL)] + [full(w) for w in ws],
        out_specs=[row(D_MODEL), pl.BlockSpec((D_MODEL, tm), lambda i: (0, i)),
                   pl.BlockSpec((n_s, tm), lambda i: (0, i))],
        out_shape=[jax.ShapeDtypeStruct((n, D_MODEL), F32), jax.ShapeDtypeStruct((D_MODEL, n), BF16),
                   jax.ShapeDtypeStruct((n_s, n), F32)],
        compiler_params=_cparams(("parallel",)),
        name="merge",
    )(x2, oa, ob, ga, gb, *ws)


def _extract_top(vals, n):
    out = []
    for i in range(n):
        m = jnp.max(vals, axis=0, keepdims=True)
        out.append(m)
        if i + 1 < n:
            vals = jnp.where(vals == m, -jnp.inf, vals)
    return out


def _peer_select_kernel(st_ref, e1_ref, e2_ref, tau_ref, *, lane_tiles):
    def head(idx, carry):
        h = idx // lane_tiles
        ls = pl.ds(pl.multiple_of((idx % lane_tiles) * LANES, LANES), LANES)
        r1 = pl.multiple_of(h * 2 * PEER_NKEYS, 2 * PEER_NKEYS)
        r2 = pl.multiple_of(h * 2 * PEER_NKEYS + PEER_NKEYS, PEER_NKEYS)
        ro = pl.multiple_of(h * PEER_NKEYS, PEER_NKEYS)
        s1 = st_ref[pl.ds(r1, PEER_NKEYS), ls]
        s2 = st_ref[pl.ds(r2, PEER_NKEYS), ls]
        v1 = _extract_top(s1, PEER_TOPK)
        v2 = _extract_top(s2, PEER_TOPK)
        v1a = jnp.concatenate(v1, axis=0)
        v2a = jnp.concatenate(v2, axis=0)
        cands = [v1[0] + v2a, v1[1] + v2a[:8], v1[2] + v2a[:8], v1[3] + v2a[:8]]
        cands += [v1[i] + v2a[:8] for i in range(4, 8)]
        cands.append(v1a[8:] + v2[0])
        cand = jnp.concatenate(cands, axis=0)
        best = _extract_top(cand, PEER_TOPK)
        z = jnp.ones_like(best[0])
        for bk in best[1:]:
            z = z + jnp.exp(bk - best[0])
        e1_ref[pl.ds(ro, PEER_NKEYS), ls] = jnp.exp(s1 - v1[0]) / z
        e2_ref[pl.ds(ro, PEER_NKEYS), ls] = jnp.exp(s2 - v2[0])
        tau_ref[pl.ds(pl.multiple_of(h * SUBLANES, SUBLANES), SUBLANES), ls] = jnp.broadcast_to(
            best[-1], (SUBLANES, LANES))
        return carry

    n_units = PEER_HEADS * lane_tiles
    per_it = 8 if n_units % 8 == 0 else 1

    def units(i, carry):
        for u in range(per_it):
            head(i * per_it + u, carry)
        return carry

    lax.fori_loop(0, n_units // per_it, units, 0)


def _peer_select(st, tn):
    n_s, n = st.shape
    n_e = PEER_HEADS * PEER_NKEYS
    return pl.pallas_call(
        functools.partial(_peer_select_kernel, lane_tiles=tn // LANES),
        grid=(n // tn,),
        in_specs=[pl.BlockSpec((n_s, tn), lambda i: (0, i))],
        out_specs=[pl.BlockSpec((n_e, tn), lambda i: (0, i)), pl.BlockSpec((n_e, tn), lambda i: (0, i)),
                   pl.BlockSpec((PEER_HEADS * SUBLANES, tn), lambda i: (0, i))],
        out_shape=[jax.ShapeDtypeStruct((n_e, n), F32), jax.ShapeDtypeStruct((n_e, n), F32),
                   jax.ShapeDtypeStruct((PEER_HEADS * SUBLANES, n), F32)],
        compiler_params=_cparams(("parallel",)),
        name="peer_select",
    )(st)


def _zero_from(x):
    u = lax.bitcast_convert_type(x, jnp.uint32)
    u = lax.shift_right_logical(lax.shift_right_logical(u, jnp.uint32(16)), jnp.uint32(16))
    return lax.bitcast_convert_type(u, F32)


def _peer_dense_kernel(xnt_ref, x1_ref, st_ref, e1_ref, e2_ref, tau_ref, u_ref, vt_ref, fg_ref, y_ref,
                       acc, hbuf0, hbuf1, act0, act1, *, eb, tn, n_blocks):
    jj = pl.program_id(1)
    hbuf, act = (hbuf0, hbuf1), (act0, act1)
    groups = eb // PEER_NKEYS
    assert groups == SUBLANES, "one aligned 8-row load of first-key scores per expert block"
    inv_sqrt2 = 1.0 / math.sqrt(2.0)

    tiles_per_it = math.gcd(PEER_LANE_TILES_PER_ITER, tn // LANES)
    n_it = tn // (tiles_per_it * LANES)
    h_rows = eb // n_it
    o_rows = D_MODEL // n_it

    n_strips_tile = PEER_NKEYS // PEER_KEY_STRIP

    def tile_rows(ls):
        a0 = (jj - 1) * SUBLANES
        s1b = [st_ref[pl.ds(pl.multiple_of(h * 2 * PEER_NKEYS + a0, SUBLANES), SUBLANES), ls]
               for h in range(PEER_HEADS)]
        e1b = [e1_ref[pl.ds(pl.multiple_of(h * PEER_NKEYS + a0, SUBLANES), SUBLANES), ls]
               for h in range(PEER_HEADS)]
        taus = [tau_ref[h * SUBLANES:h * SUBLANES + 1, ls] for h in range(PEER_HEADS)]
        return s1b, e1b, taus

    def mixture_strip(slot, ls, rows, si, pin):
        s1b, e1b, taus = rows
        b0 = si * PEER_KEY_STRIP
        w = [jnp.zeros((PEER_KEY_STRIP, LANES), F32) for _ in range(groups)]
        for h in range(PEER_HEADS):
            r2 = h * 2 * PEER_NKEYS + PEER_NKEYS + b0
            s2 = st_ref[r2:r2 + PEER_KEY_STRIP, ls]
            e2 = e2_ref[h * PEER_NKEYS + b0:h * PEER_NKEYS + b0 + PEER_KEY_STRIP, ls]
            for ai in range(groups):
                keep = s1b[h][ai:ai + 1, :] + s2 >= taus[h]
                w[ai] = w[ai] + jnp.where(keep, e2 * e1b[h][ai:ai + 1, :], 0.0)
        for ai in range(groups):
            es = slice(ai * PEER_NKEYS + b0, ai * PEER_NKEYS + b0 + PEER_KEY_STRIP)
            h_t = hbuf[slot][es, ls]
            gelu = 0.5 * h_t * (1.0 + lax.erf(h_t * inv_sqrt2))
            val = w[ai] * gelu
            if ai == groups - 1:
                val = val + jnp.concatenate([pin] * (PEER_KEY_STRIP // SUBLANES), axis=0)
            act[slot][es, ls] = val.astype(BF16)

    def sweep(slot, do_hidden, do_mixture, do_output):
        n_strips = tiles_per_it * n_strips_tile
        half = n_strips // 2
        ch_h, ch_o = h_rows // half, o_rows // half

        def step(it, carry):
            h0 = pl.multiple_of(it * h_rows, h_rows)
            o0 = pl.multiple_of(it * o_rows, o_rows)
            r_h = r_o = None
            if do_hidden:
                r_h = jnp.dot(u_ref[pl.ds(h0, h_rows), :], xnt_ref[...], preferred_element_type=F32)
            if do_output:
                r_o = jnp.dot(vt_ref[pl.ds(o0, o_rows), :], act[1 - slot][...], preferred_element_type=F32)
            if not do_mixture:
                if do_hidden:
                    hbuf[1 - slot][pl.ds(h0, h_rows), :] = r_h
                if do_output:
                    acc[pl.ds(o0, o_rows), :] += r_o
                return carry
            for idx in range(n_strips):
                k, si = divmod(idx, n_strips_tile)
                ls = pl.ds(pl.multiple_of((it * tiles_per_it + k) * LANES, LANES), LANES)
                if si == 0:
                    rows = tile_rows(ls)
                if idx < half:
                    c = idx * ch_h
                    mixture_strip(slot, ls, rows, si, _zero_from(r_h[c:c + SUBLANES, 0:LANES]))
                    hbuf[1 - slot][pl.ds(pl.multiple_of(h0 + c, ch_h), ch_h), :] = r_h[c:c + ch_h, :]
                else:
                    c = (idx - half) * ch_o
                    mixture_strip(slot, ls, rows, si, _zero_from(r_o[c:c + SUBLANES, 0:LANES]))
                    acc[pl.ds(pl.multiple_of(o0 + c, ch_o), ch_o), :] += r_o[c:c + ch_o, :]
            return carry
        lax.fori_loop(0, n_it, step, 0)

    @pl.when(jj == 0)
    def _():
        acc[...] = jnp.zeros(acc.shape, F32)
        act[0][...] = jnp.zeros(act[0].shape, BF16)
        sweep(0, True, False, False)

    for slot in range(2):
        @pl.when((jj >= 1) & (jj <= n_blocks) & (jj % 2 == slot))
        def _(slot=slot):
            sweep(slot, True, True, True)

    @pl.when(jj == n_blocks + 1)
    def _():
        sweep((n_blocks + 1) % 2, False, False, True)
        y = x1_ref[...] + acc[...].T
        y_ref[...] = _rms(y, fg_ref[...])


def _peer_dense(xnt, x1, st, e1, e2, tau, u_bf, vt_bf, final_g, tn, eb):
    n = x1.shape[0]
    n_blocks = PEER_N // eb
    tok = lambda w: pl.BlockSpec((tn, w), lambda i, j: (i, 0))
    col = lambda a: pl.BlockSpec((a.shape[0], tn), lambda i, j: (0, i))
    return pl.pallas_call(
        functools.partial(_peer_dense_kernel, eb=eb, tn=tn, n_blocks=n_blocks),
        grid=(n // tn, n_blocks + 2),
        in_specs=[col(xnt), tok(D_MODEL), col(st), col(e1), col(e2), col(tau),
                  pl.BlockSpec((eb, D_MODEL), lambda i, j: (jnp.minimum(j, n_blocks - 1), 0)),
                  pl.BlockSpec((D_MODEL, eb), lambda i, j: (0, jnp.clip(j - 2, 0, n_blocks - 1))),
                  pl.BlockSpec((1, D_MODEL), lambda i, j: (0, 0))],
        out_specs=tok(D_MODEL),
        out_shape=jax.ShapeDtypeStruct((n, D_MODEL), F32),
        scratch_shapes=[pltpu.VMEM((D_MODEL, tn), F32), pltpu.VMEM((eb, tn), F32), pltpu.VMEM((eb, tn), F32),
                        pltpu.VMEM((eb, tn), BF16), pltpu.VMEM((eb, tn), BF16)],
        compiler_params=_cparams(("parallel", "arbitrary")),
        name="peer_dense",
    )(xnt, x1, st, e1, e2, tau, u_bf, vt_bf, final_g.reshape(1, -1))


def _group(x, s0, moba_fn, wts, tm, tn, eb):
    (n1, w_in, wa2, ba, gla_g, w_pa, w_pb, w_out, n2, wq, keys, u_bf, vt_bf, final_g) = wts
    bsz, t, _ = x.shape
    n = bsz * t
    x2 = x.reshape(n, D_MODEL)
    qa, ka, va, ra, la, qb, kb, vb, ga, gb = _in_proj(x2, n1, w_in, wa2, ba, tm)
    r3 = lambda a: a.reshape(bsz, t, a.shape[-1])
    oa, s_fin = _gla(r3(qa), r3(ka), r3(va), r3(la), r3(ra), s0, gla_g)
    ob = moba_fn(r3(qb), r3(kb), r3(vb))
    x1, xn, st = _merge(x2, oa.reshape(n, GLA_V_W), ob.reshape(n, MOBA_W), ga, gb,
                        w_pa, w_pb, w_out, n2, wq, keys, tm)
    e1, e2, tau = _peer_select(st, tn)
    y = _peer_dense(xn, x1, st, e1, e2, tau, u_bf, vt_bf, final_g, tn, eb)
    kv_shape = (1, bsz, t, MOBA_HEADS, MOBA_DH)
    return y.reshape(bsz, t, D_MODEL), kb.reshape(kv_shape), vb.reshape(kv_shape), s_fin[None]


def kernel(x_prompt, x_sample, cache_k, cache_v, state_gla, page_table, norm1_g, w_in, gla_wa2, gla_ba,
           gla_norm_g, w_pa, w_pb, w_out, norm2_g, peer_wq, peer_keys, peer_u, peer_v, final_g):
    assert w_in.shape[0] == 1, "single-layer step"
    u_bf = peer_u[0].astype(BF16)
    vt_bf = peer_v[0].astype(BF16).T
    wts = (norm1_g[0], w_in[0], gla_wa2[0], gla_ba[0], gla_norm_g[0], w_pa[0], w_pb[0], w_out[0],
           norm2_g[0], peer_wq[0], peer_keys[0], u_bf, vt_bf, final_g)
    bp = x_prompt.shape[0]
    s0p = jnp.zeros((bp, GLA_HEADS, GLA_DK, GLA_DV), F32)
    n_s = x_sample.shape[0] * x_sample.shape[1]
    tok_blk = lambda n, pref: pref if n % pref == 0 else n
    n_p = bp * x_prompt.shape[1]
    yp, kp, vp, sp = _group(x_prompt, s0p, _moba_prompt, wts,
                            tok_blk(n_p, 256), tok_blk(n_p, 512), PEER_EXPERT_BLOCK)
    ck, cv = cache_k[0], cache_v[0]
    ys, ks, vs, ss = _group(x_sample, state_gla[0],
                            lambda q, k, v: _moba_sample(q, k, v, ck, cv, page_table), wts,
                            tok_blk(n_s, 256), tok_blk(n_s, 512), PEER_EXPERT_BLOCK)
    return (yp, ys, kp, vp, sp, ks, vs, ss)
```

```python
import functools
import math

import jax
import jax.numpy as jnp
from jax import lax
from jax.experimental import pallas as pl
from jax.experimental.pallas import tpu as pltpu

F32 = jnp.float32
BF16 = jnp.bfloat16

D_MODEL = 1024
GLA_HEADS = 4
GLA_DK = 64
GLA_DV = 128
GLA_RANK = 16
GLA_TAU = 16.0
GLA_CHUNK = 64
GLA_QK_W = GLA_HEADS * GLA_DK
GLA_V_W = GLA_HEADS * GLA_DV
MOBA_HEADS = 8
MOBA_DH = 64
MOBA_W = MOBA_HEADS * MOBA_DH
MOBA_BLOCK = 256
MOBA_TOPK = 3
MOBA_QCHUNK = 128
NEG = -1e30
PEER_HEADS = 8
PEER_NKEYS = 128
PEER_N = PEER_NKEYS * PEER_NKEYS
PEER_QDIM = 256
PEER_HALF = PEER_QDIM // 2
PEER_TOPK = 16
EPS = 1e-6
IN_SIZES = (GLA_QK_W, GLA_QK_W, GLA_V_W, GLA_V_W, GLA_RANK, MOBA_W, MOBA_W, MOBA_W, D_MODEL, D_MODEL)

LANES = 128
SUBLANES = 8
PEER_EXPERT_BLOCK = SUBLANES * PEER_NKEYS
PEER_KEY_STRIP = 16
PEER_LANE_TILES_PER_ITER = 2
VMEM_LIMIT = 56 * 1024 * 1024

_NT = (((1,), (1,)), ((), ()))
_TN = (((0,), (0,)), ((), ()))


def _cparams(sem):
    return pltpu.CompilerParams(dimension_semantics=sem, vmem_limit_bytes=VMEM_LIMIT)


def _rms(x, g):
    return x * lax.rsqrt(jnp.mean(x * x, axis=-1, keepdims=True) + EPS) * g


def _inproj_kernel(x_ref, g_ref, wqa, wka, wva, wra, wlr, wa2, ba, wqb, wkb, wvb, wga, wgb,
                   qa, ka, va, ra, la, qb, kb, vb, ga, gb):
    xn = _rms(x_ref[...], g_ref[...]).astype(BF16)
    for w, o in ((wqa, qa), (wka, ka), (wva, va), (wra, ra), (wqb, qb), (wkb, kb), (wvb, vb),
                 (wga, ga), (wgb, gb)):
        o[...] = jnp.dot(xn, w[...], preferred_element_type=F32).astype(o.dtype)
    lr = jnp.dot(xn, wlr[...], preferred_element_type=F32).astype(BF16)
    z = jnp.dot(lr, wa2[...], preferred_element_type=F32) + ba[...]
    la[...] = (jnp.minimum(z, 0.0) - jnp.log1p(jnp.exp(-jnp.abs(z)))) * (1.0 / GLA_TAU)


def _in_proj(x2, n1, w_in, wa2, ba, tm):
    n = x2.shape[0]
    offs = [0]
    for c in IN_SIZES:
        offs.append(offs[-1] + c)
    wb = w_in.astype(BF16)
    piece = lambda i: wb[:, offs[i]:offs[i + 1]]
    wlr = jnp.pad(piece(4), ((0, 0), (0, LANES - GLA_RANK)))
    wa2p = jnp.pad(wa2.astype(BF16), ((0, LANES - GLA_RANK), (0, 0)))
    weights = [piece(0), piece(1), piece(2), piece(3), wlr, wa2p, ba.reshape(1, -1),
               piece(5), piece(6), piece(7), piece(8), piece(9)]
    widths = [GLA_QK_W, GLA_QK_W, GLA_V_W, GLA_V_W, GLA_QK_W, MOBA_W, MOBA_W, MOBA_W, D_MODEL, D_MODEL]
    row = lambda w: pl.BlockSpec((tm, w), lambda i: (i, 0))
    full = lambda a: pl.BlockSpec(a.shape, lambda i: (0, 0))
    return pl.pallas_call(
        _inproj_kernel,
        grid=(n // tm,),
        in_specs=[row(D_MODEL), full(n1.reshape(1, -1))] + [full(w) for w in weights],
        out_specs=[row(w) for w in widths],
        out_shape=[jax.ShapeDtypeStruct((n, w), F32) for w in widths],
        compiler_params=_cparams(("parallel",)),
        name="in_proj",
    )(x2, n1.reshape(1, -1), *weights)


def _gla_kernel(qa, ka, va, la, ra, s0, g_ref, oa, sfin, s_scr, *, chunk, n_chunks, seqs):
    t = pl.program_id(1)

    @pl.when(t == 0)
    def _():
        s_scr[...] = s0[...]

    ri = lax.broadcasted_iota(jnp.int32, (chunk, chunk), 0)
    ci = lax.broadcasted_iota(jnp.int32, (chunk, chunk), 1)
    causal = ci <= ri
    tril = causal.astype(F32)
    eye_k = (lax.broadcasted_iota(jnp.int32, (GLA_DK, GLA_DK), 0)
             == lax.broadcasted_iota(jnp.int32, (GLA_DK, GLA_DK), 1))
    ones_kv = jnp.ones((GLA_DK, GLA_DV), F32)
    g = g_ref[...]

    def body(c, carry):
        r0 = pl.multiple_of(c * chunk, chunk)
        rows = pl.ds(r0, chunk)
        for bi in range(seqs):
            a = la[bi, rows, :]
            b = jnp.dot(tril, a, preferred_element_type=F32, precision=lax.Precision.HIGHEST)
            b_last = b[chunk - 1:chunk, :]
            q = qa[bi, rows, :] * (GLA_DK ** -0.5)
            k = ka[bi, rows, :]
            q_dec = (q * jnp.exp(b)).astype(BF16)
            k_inv = (k * jnp.exp(-b)).astype(BF16)
            k_up = (k * jnp.exp(b_last - b)).astype(BF16)
            dec = jnp.exp(b_last)
            outs = []
            for h in range(GLA_HEADS):
                ks = slice(h * GLA_DK, (h + 1) * GLA_DK)
                vs = slice(h * GLA_DV, (h + 1) * GLA_DV)
                s = s_scr[bi, h]
                v = va[bi, rows, vs].astype(BF16)
                o = jnp.dot(q_dec[:, ks], s.astype(BF16), preferred_element_type=F32)
                att = lax.dot_general(q_dec[:, ks], k_inv[:, ks], _NT, preferred_element_type=F32)
                att = jnp.where(causal, att, 0.0).astype(BF16)
                o = o + jnp.dot(att, v, preferred_element_type=F32)
                kv = lax.dot_general(k_up[:, ks], v, _TN, preferred_element_type=F32)
                dmat = jnp.where(eye_k, jnp.broadcast_to(dec[:, ks], (GLA_DK, GLA_DK)), 0.0)
                dcol = jnp.dot(dmat, ones_kv, preferred_element_type=F32, precision=lax.Precision.HIGHEST)
                s_scr[bi, h] = dcol * s + kv
                on = _rms(o, g)
                r = ra[bi, rows, vs]
                outs.append(on * (r * jax.nn.sigmoid(r)))
            oa[bi, rows, :] = jnp.concatenate(outs, axis=-1).astype(oa.dtype)
        return carry

    lax.fori_loop(0, n_chunks, body, 0)

    @pl.when(t == pl.num_programs(1) - 1)
    def _():
        sfin[...] = s_scr[...]


GLA_MIN_CHUNK = 16
GLA_SEQS_PER_STEP = 2


def _gla(qa, ka, va, la, ra, s0, gla_g):
    bsz, t_real, _ = qa.shape
    chunk = math.gcd(GLA_CHUNK, t_real)
    if chunk < GLA_MIN_CHUNK:
        chunk = GLA_MIN_CHUNK
        pad = (-t_real) % chunk
        qa, ka, va, la, ra = (jnp.pad(a, ((0, 0), (0, pad), (0, 0))) for a in (qa, ka, va, la, ra))
    t = qa.shape[1]
    tb = min(t, 512)
    seqs = math.gcd(GLA_SEQS_PER_STEP, bsz)
    blk = lambda w: pl.BlockSpec((seqs, tb, w), lambda b, i: (b, i, 0))
    st = pl.BlockSpec((seqs, GLA_HEADS, GLA_DK, GLA_DV), lambda b, i: (b, 0, 0, 0))
    oa, s_fin = pl.pallas_call(
        functools.partial(_gla_kernel, chunk=chunk, n_chunks=tb // chunk, seqs=seqs),
        grid=(bsz // seqs, t // tb),
        in_specs=[blk(GLA_QK_W), blk(GLA_QK_W), blk(GLA_V_W), blk(GLA_QK_W), blk(GLA_V_W), st,
                  pl.BlockSpec((1, GLA_DV), lambda b, i: (0, 0))],
        out_specs=[blk(GLA_V_W), st],
        out_shape=[jax.ShapeDtypeStruct((bsz, t, GLA_V_W), BF16),
                   jax.ShapeDtypeStruct((bsz, GLA_HEADS, GLA_DK, GLA_DV), F32)],
        scratch_shapes=[pltpu.VMEM((seqs, GLA_HEADS, GLA_DK, GLA_DV), F32)],
        compiler_params=_cparams(("parallel", "arbitrary")),
        name="gla",
    )(qa, ka, va, la, ra, s0, gla_g.reshape(1, -1))
    return oa[:, :t_real], s_fin


def _alibi_slopes():
    return jnp.exp2(-8.0 * jnp.arange(1, MOBA_HEADS + 1, dtype=F32) / MOBA_HEADS)


def _topk_mask(gm, n_blocks):
    lane = lax.broadcasted_iota(jnp.int32, gm.shape, 1)
    rank = jnp.zeros(gm.shape, F32)
    for j in range(n_blocks):
        col = gm[:, j:j + 1]
        ahead = (col > gm) | ((col == gm) & (j < lane))
        rank = rank + jnp.where(ahead, 1.0, 0.0)
    return rank < float(MOBA_TOPK)


def _topk_mask_t(gm, n_blocks):
    row = lax.broadcasted_iota(jnp.int32, gm.shape, 0)
    rank = jnp.zeros(gm.shape, F32)
    for j in range(n_blocks):
        other = gm[j:j + 1, :]
        ahead = (other > gm) | ((other == gm) & (j < row))
        rank = rank + jnp.where(ahead, 1.0, 0.0)
    return rank < float(MOBA_TOPK)


def _moba_prompt_kernel(slopes_ref, q_ref, k_ref, v_ref, o_ref, *, seq):
    nb = seq // MOBA_BLOCK
    qrows = MOBA_BLOCK
    heads = LANES // MOBA_DH
    scale = MOBA_DH ** -0.5
    hp = pl.program_id(1)
    lane_q = lax.broadcasted_iota(jnp.int32, (qrows, LANES), 1)
    rel0 = (lax.broadcasted_iota(jnp.int32, (qrows, MOBA_BLOCK), 0)
            - lax.broadcasted_iota(jnp.int32, (qrows, MOBA_BLOCK), 1))
    causal = rel0 >= 0
    nb_pad = -(-nb // SUBLANES) * SUBLANES
    assert heads == 2 and nb_pad <= MOBA_DH, "the other head's lanes must hold one bias lane per block"
    spare = [(1 - x) * MOBA_DH for x in range(heads)]
    blk_t = lax.broadcasted_iota(jnp.int32, (nb_pad, qrows), 0)
    hmask = [lane_q // MOBA_DH == x for x in range(heads)]
    slopes = [slopes_ref[hp * heads + x] for x in range(heads)]
    srel = [slopes[x] * rel0.astype(F32) for x in range(heads)]
    kmean = jnp.concatenate(
        [jnp.mean(k_ref[0, j * MOBA_BLOCK:(j + 1) * MOBA_BLOCK, :], axis=0, keepdims=True)
         for j in range(nb)] + [jnp.zeros((LANES - nb, LANES), F32)], axis=0)

    def q_block(cur, carry):
        r0 = pl.multiple_of(cur * qrows, qrows)
        q = q_ref[0, pl.ds(r0, qrows), :]
        past_t = blk_t < cur
        qx = []
        for x in range(heads):
            qh = jnp.where(hmask[x], q, 0.0)
            gate_t = lax.dot_general(kmean, qh, _NT, preferred_element_type=F32,
                                     precision=lax.Precision.HIGHEST)[:nb_pad, :]
            keep_t = past_t & _topk_mask_t(jnp.where(past_t, gate_t, NEG), nb)
            pieces = [jnp.zeros((spare[x], qrows), F32), jnp.where(keep_t, 0.0, NEG),
                      jnp.zeros((LANES - spare[x] - nb_pad, qrows), F32)]
            bias_t = jnp.concatenate([p for p in pieces if p.shape[0]], axis=0)
            qx.append(jnp.where(hmask[x], q, bias_t.T).astype(BF16))

        kf = k_ref[0, pl.ds(r0, qrows), :]
        vb = v_ref[0, pl.ds(r0, qrows), :].astype(BF16)
        init = []
        for x in range(heads):
            kb = jnp.where(hmask[x], kf, 0.0).astype(BF16)
            s = lax.dot_general(qx[x], kb, _NT, preferred_element_type=F32) * scale - srel[x]
            s = jnp.where(causal, s, NEG)
            m = jnp.max(s, axis=-1, keepdims=True)
            p = jnp.exp(s - m)
            init.append((m, jnp.sum(p, axis=-1, keepdims=True),
                         jnp.dot(p.astype(BF16), vb, preferred_element_type=F32)))

        def past_block(j, st):
            k0 = pl.multiple_of(j * MOBA_BLOCK, MOBA_BLOCK)
            kf = k_ref[0, pl.ds(k0, MOBA_BLOCK), :]
            vb = v_ref[0, pl.ds(k0, MOBA_BLOCK), :].astype(BF16)
            off = (r0 - k0).astype(F32)
            out = []
            for x in range(heads):
                m, l, acc = st[x]
                one_hot = jnp.where(lane_q == j + spare[x], 1.0, 0.0)
                kb = jnp.where(hmask[x], kf, one_hot).astype(BF16)
                s = lax.dot_general(qx[x], kb, _NT, preferred_element_type=F32) * scale
                s = s - (srel[x] + slopes[x] * off)
                m_new = jnp.maximum(m, jnp.max(s, axis=-1, keepdims=True))
                alpha = jnp.exp(m - m_new)
                p = jnp.exp(s - m_new)
                acc = alpha * acc + jnp.dot(p.astype(BF16), vb, preferred_element_type=F32)
                out.append((m_new, alpha * l + jnp.sum(p, axis=-1, keepdims=True), acc))
            return tuple(out)

        st = lax.fori_loop(0, cur, past_block, tuple(init))
        o = st[heads - 1][2] / st[heads - 1][1]
        for x in range(heads - 2, -1, -1):
            o = jnp.where(hmask[x], st[x][2] / st[x][1], o)
        o_ref[0, pl.ds(r0, qrows), :] = o.astype(o_ref.dtype)
        return carry

    lax.fori_loop(0, nb, q_block, 0)


def _moba_prompt(qb, kb, vb):
    bsz, t, _ = qb.shape
    blk = pl.BlockSpec((1, t, LANES), lambda b, h: (b, 0, h))
    return pl.pallas_call(
        functools.partial(_moba_prompt_kernel, seq=t),
        grid=(bsz, MOBA_W // LANES),
        in_specs=[pl.BlockSpec(memory_space=pltpu.SMEM), blk, blk, blk],
        out_specs=blk,
        out_shape=jax.ShapeDtypeStruct((bsz, t, MOBA_W), BF16),
        compiler_params=_cparams(("parallel", "parallel")),
        name="moba_prompt",
    )(_alibi_slopes(), qb, kb, vb)


def _moba_sample_kernel(pt_ref, slopes_ref, q_ref, kn_ref, vn_ref, *rest, n_new, n_past_blocks, past_len,
                        blocks_per_step, pages_per_block):
    del pt_ref
    n_pg = blocks_per_step * pages_per_block
    k_refs, v_refs = rest[:n_pg], rest[n_pg:2 * n_pg]
    o_ref, qbd, gate_s, m_s, l_s, o_s = rest[2 * n_pg:]
    j = pl.program_id(1)
    rows = n_new * MOBA_HEADS
    page = k_refs[0].shape[2]
    scale = MOBA_DH ** -0.5
    row_i = lax.broadcasted_iota(jnp.int32, (rows, 1), 0)
    head_r = row_i % MOBA_HEADS
    tok_r = row_i // MOBA_HEADS
    slope = jnp.zeros((rows, 1), F32)
    for h in range(MOBA_HEADS):
        slope = jnp.where(head_r == h, slopes_ref[h], slope)
    head_mask = (lax.broadcasted_iota(jnp.int32, (rows, MOBA_W), 1) // MOBA_DH
                 == lax.broadcasted_iota(jnp.int32, (rows, MOBA_W), 0) % MOBA_HEADS)
    key_i = lax.broadcasted_iota(jnp.int32, (rows, page), 1)
    lane_nb = lax.broadcasted_iota(jnp.int32, (rows, LANES), 1)

    @pl.when(j == 0)
    def _():
        q = q_ref[0]
        qrep = jnp.concatenate([jnp.broadcast_to(q[t:t + 1, :], (MOBA_HEADS, MOBA_W))
                                for t in range(n_new)], axis=0)
        qbd[...] = jnp.where(head_mask, qrep, 0.0)
        gate_s[...] = jnp.full(gate_s.shape, NEG, F32)
        m_s[...] = jnp.full(m_s.shape, NEG, F32)
        l_s[...] = jnp.zeros(l_s.shape, F32)

    qf = qbd[...]
    q16 = qf.astype(BF16)
    gate_all, m_all, l_all = gate_s[...], m_s[...], l_s[...]
    kt_all = jnp.concatenate([r[0].astype(BF16) for r in k_refs], axis=1)
    raw_all = jnp.dot(q16, kt_all, preferred_element_type=F32)
    for bi in range(blocks_per_step):
        blk = j * blocks_per_step + bi
        ss = []
        gate_j = jnp.zeros((rows, 1), F32)
        for o in range(pages_per_block):
            c0 = (bi * pages_per_block + o) * page
            raw = raw_all[:, c0:c0 + page]
            gate_j = gate_j + jnp.sum(raw, axis=-1, keepdims=True)
            dist = (past_len - blk * MOBA_BLOCK - o * page) + tok_r - key_i
            ss.append(raw * scale - slope * dist.astype(F32))
        gate_j = gate_j * (1.0 / MOBA_BLOCK)
        m = jnp.max(ss[0], axis=-1, keepdims=True)
        for s in ss[1:]:
            m = jnp.maximum(m, jnp.max(s, axis=-1, keepdims=True))
        l = jnp.zeros((rows, 1), F32)
        o_blk = jnp.zeros((rows, MOBA_W), F32)
        for o, s in enumerate(ss):
            p = jnp.exp(s - m)
            l = l + jnp.sum(p, axis=-1, keepdims=True)
            vt = v_refs[bi * pages_per_block + o][0].astype(BF16)
            o_blk = o_blk + lax.dot_general(p.astype(BF16), vt, _NT, preferred_element_type=F32)
        here = lane_nb == blk
        gate_all = jnp.where(here, gate_j, gate_all)
        m_all = jnp.where(here, m, m_all)
        l_all = jnp.where(here, l, l_all)
        o_s[blk] = o_blk
    gate_s[...] = gate_all
    m_s[...] = m_all
    l_s[...] = l_all

    @pl.when(j == pl.num_programs(1) - 1)
    def _():
        sel = _topk_mask(gate_s[...], n_past_blocks) & (lane_nb < n_past_blocks)
        kn = kn_ref[0]
        vn = vn_ref[0]
        s_own = []
        for c in range(n_new):
            sc = jnp.sum(qf * kn[c:c + 1, :], axis=-1, keepdims=True) * scale
            sc = sc - slope * (tok_r - c).astype(F32)
            s_own.append(jnp.where(tok_r >= c, sc, NEG))
        m_sel = jnp.where(sel, m_s[...], NEG)
        m_tot = jnp.max(m_sel, axis=-1, keepdims=True)
        for sc in s_own:
            m_tot = jnp.maximum(m_tot, sc)
        w = jnp.where(sel, jnp.exp(m_sel - m_tot), 0.0)
        denom = jnp.sum(w * l_s[...], axis=-1, keepdims=True)
        num = jnp.zeros((rows, MOBA_W), F32)
        for c, sc in enumerate(s_own):
            pc = jnp.exp(sc - m_tot)
            denom = denom + pc
            num = num + pc * vn[c:c + 1, :]

        def add_block(jj, acc):
            wj = jnp.sum(jnp.where(lane_nb == jj, w, 0.0), axis=-1, keepdims=True)
            return acc + wj * o_s[jj]

        num = lax.fori_loop(0, n_past_blocks, add_block, num)
        res = jnp.where(head_mask, num / denom, 0.0)
        o_ref[0] = jnp.concatenate(
            [jnp.sum(res[t * MOBA_HEADS:(t + 1) * MOBA_HEADS, :], axis=0, keepdims=True)
             for t in range(n_new)], axis=0).astype(o_ref.dtype)


MOBA_SAMPLE_BLOCKS_PER_STEP = 8


def _moba_sample(qb, kb, vb, cache_k, cache_v, page_table):
    bsz, n_new, _ = qb.shape
    n_pages = page_table.shape[1]
    page = cache_k.shape[1]
    ppb = MOBA_BLOCK // page
    assert ppb * page == MOBA_BLOCK and n_pages % ppb == 0
    nbp = n_pages // ppb
    bps = MOBA_SAMPLE_BLOCKS_PER_STEP if nbp % MOBA_SAMPLE_BLOCKS_PER_STEP == 0 else 1
    assert nbp <= LANES
    rows = n_new * MOBA_HEADS
    to_pages = lambda c: jnp.transpose(c, (0, 2, 3, 1)).reshape(c.shape[0], MOBA_W, page)
    new = pl.BlockSpec((1, n_new, MOBA_W), lambda b, j, pt: (b, 0, 0))
    n_pg = bps * ppb
    pg = lambda o: pl.BlockSpec((1, MOBA_W, page), lambda b, j, pt: (pt[b, n_pg * j + o], 0, 0))
    grid_spec = pltpu.PrefetchScalarGridSpec(
        num_scalar_prefetch=1,
        grid=(bsz, nbp // bps),
        in_specs=[pl.BlockSpec(memory_space=pltpu.SMEM), new, new, new]
                 + [pg(o) for o in range(n_pg)] + [pg(o) for o in range(n_pg)],
        out_specs=new,
        scratch_shapes=[pltpu.VMEM((rows, MOBA_W), F32), pltpu.VMEM((rows, LANES), F32),
                        pltpu.VMEM((rows, LANES), F32), pltpu.VMEM((rows, LANES), F32),
                        pltpu.VMEM((nbp, rows, MOBA_W), F32)],
    )
    ck, cv = to_pages(cache_k), to_pages(cache_v)
    return pl.pallas_call(
        functools.partial(_moba_sample_kernel, n_new=n_new, n_past_blocks=nbp, past_len=n_pages * page,
                          blocks_per_step=bps, pages_per_block=ppb),
        grid_spec=grid_spec,
        out_shape=jax.ShapeDtypeStruct((bsz, n_new, MOBA_W), BF16),
        compiler_params=_cparams(("parallel", "arbitrary")),
        name="moba_sample",
    )(page_table, _alibi_slopes(), qb, kb, vb, *([ck] * n_pg), *([cv] * n_pg))


def _merge_kernel(x_ref, oa_ref, ob_ref, ga_ref, gb_ref, wpa, wpb, wout, n2, wq, keys,
                  x1_ref, xnt_ref, st_ref):
    ya = jnp.dot(oa_ref[...], wpa[...], preferred_element_type=F32)
    yb = jnp.dot(ob_ref[...], wpb[...], preferred_element_type=F32)
    merged = jax.nn.sigmoid(ga_ref[...]) * ya + jax.nn.sigmoid(gb_ref[...]) * yb
    x1 = x_ref[...] + jnp.dot(merged.astype(BF16), wout[...], preferred_element_type=F32)
    x1_ref[...] = x1
    xn = _rms(x1, n2[...])
    xnt_ref[...] = xn.T.astype(BF16)
    q = jnp.dot(xn.astype(BF16), wq[...], preferred_element_type=F32).astype(BF16)
    for i in range(2 * PEER_HEADS):
        st_ref[i * PEER_NKEYS:(i + 1) * PEER_NKEYS, :] = lax.dot_general(
            keys[i], q[:, i * PEER_HALF:(i + 1) * PEER_HALF], _NT, preferred_element_type=F32)


def _merge(x2, oa, ob, ga, gb, w_pa, w_pb, w_out, n2, wq, keys, tm):
    n = x2.shape[0]
    ws = [w_pa.astype(BF16), w_pb.astype(BF16), w_out.astype(BF16), n2.reshape(1, -1), wq.astype(BF16),
          keys.astype(BF16).reshape(2 * PEER_HEADS, PEER_NKEYS, PEER_HALF)]
    row = lambda w: pl.BlockSpec((tm, w), lambda i: (i, 0))
    full = lambda a: pl.BlockSpec(a.shape, lambda i: (0,) * a.ndim)
    n_s = 2 * PEER_HEADS * PEER_NKEYS
    return pl.pallas_call(
        _merge_kernel,
        grid=(n // tm,),
        in_specs=[row(D_MODEL), row(GLA_V_W), row(MOBA_W), row(D_MODEL), row(D_MODEL)] + [full(w) for w in ws],
        out_specs=[row(D_MODEL), pl.BlockSpec((D_MODEL, tm), lambda i: (0, i)),
                   pl.BlockSpec((n_s, tm), lambda i: (0, i))],
        out_shape=[jax.ShapeDtypeStruct((n, D_MODEL), F32), jax.ShapeDtypeStruct((D_MODEL, n), BF16),
                   jax.ShapeDtypeStruct((n_s, n), F32)],
        compiler_params=_cparams(("parallel",)),
        name="merge",
    )(x2, oa, ob, ga, gb, *ws)


def _extract_top(vals, n):
    out = []
    for i in range(n):
        m = jnp.max(vals, axis=0, keepdims=True)
        out.append(m)
        if i + 1 < n:
            vals = jnp.where(vals == m, -jnp.inf, vals)
    return out


def _peer_select_kernel(st_ref, e1_ref, e2_ref, tau_ref, *, lane_tiles):
    def head(idx, carry):
        h = idx // lane_tiles
        ls = pl.ds(pl.multiple_of((idx % lane_tiles) * LANES, LANES), LANES)
        r1 = pl.multiple_of(h * 2 * PEER_NKEYS, 2 * PEER_NKEYS)
        r2 = pl.multiple_of(h * 2 * PEER_NKEYS + PEER_NKEYS, PEER_NKEYS)
        ro = pl.multiple_of(h * PEER_NKEYS, PEER_NKEYS)
        s1 = st_ref[pl.ds(r1, PEER_NKEYS), ls]
        s2 = st_ref[pl.ds(r2, PEER_NKEYS), ls]
        v1 = _extract_top(s1, PEER_TOPK)
        v2 = _extract_top(s2, PEER_TOPK)
        v1a = jnp.concatenate(v1, axis=0)
        v2a = jnp.concatenate(v2, axis=0)
        cands = [v1[0] + v2a, v1[1] + v2a[:8], v1[2] + v2a[:8], v1[3] + v2a[:8]]
        cands += [v1[i] + v2a[:8] for i in range(4, 8)]
        cands.append(v1a[8:] + v2[0])
        cand = jnp.concatenate(cands, axis=0)
        best = _extract_top(cand, PEER_TOPK)
        z = jnp.ones_like(best[0])
        for bk in best[1:]:
            z = z + jnp.exp(bk - best[0])
        e1_ref[pl.ds(ro, PEER_NKEYS), ls] = jnp.exp(s1 - v1[0]) / z
        e2_ref[pl.ds(ro, PEER_NKEYS), ls] = jnp.exp(s2 - v2[0])
        tau_ref[pl.ds(pl.multiple_of(h * SUBLANES, SUBLANES), SUBLANES), ls] = jnp.broadcast_to(
            best[-1], (SUBLANES, LANES))
        return carry

    n_units = PEER_HEADS * lane_tiles
    per_it = 8 if n_units % 8 == 0 else 1

    def units(i, carry):
        for u in range(per_it):
            head(i * per_it + u, carry)
        return carry

    lax.fori_loop(0, n_units // per_it, units, 0)


def _peer_select(st, tn):
    n_s, n = st.shape
    n_e = PEER_HEADS * PEER_NKEYS
    return pl.pallas_call(
        functools.partial(_peer_select_kernel, lane_tiles=tn // LANES),
        grid=(n // tn,),
        in_specs=[pl.BlockSpec((n_s, tn), lambda i: (0, i))],
        out_specs=[pl.BlockSpec((n_e, tn), lambda i: (0, i)), pl.BlockSpec((n_e, tn), lambda i: (0, i)),
                   pl.BlockSpec((PEER_HEADS * SUBLANES, tn), lambda i: (0, i))],
        out_shape=[jax.ShapeDtypeStruct((n_e, n), F32), jax.ShapeDtypeStruct((n_e, n), F32),
                   jax.ShapeDtypeStruct((PEER_HEADS * SUBLANES, n), F32)],
        compiler_params=_cparams(("parallel",)),
        name="peer_select",
    )(st)


def _zero_from(x):
    u = lax.bitcast_convert_type(x, jnp.uint32)
    u = lax.shift_right_logical(lax.shift_right_logical(u, jnp.uint32(16)), jnp.uint32(16))
    return lax.bitcast_convert_type(u, F32)


def _peer_dense_kernel(xnt_ref, x1_ref, st_ref, e1_ref, e2_ref, tau_ref, u_ref, vt_ref, fg_ref, y_ref,
                       acc, hbuf0, hbuf1, act0, act1, *, eb, tn, n_blocks):
    jj = pl.program_id(1)
    hbuf, act = (hbuf0, hbuf1), (act0, act1)
    groups = eb // PEER_NKEYS
    assert groups == SUBLANES, "one aligned 8-row load of first-key scores per expert block"
    inv_sqrt2 = 1.0 / math.sqrt(2.0)

    tiles_per_it = math.gcd(PEER_LANE_TILES_PER_ITER, tn // LANES)
    n_it = tn // (tiles_per_it * LANES)
    h_rows = eb // n_it
    o_rows = D_MODEL // n_it

    n_strips_tile = PEER_NKEYS // PEER_KEY_STRIP

    def tile_rows(ls):
        a0 = (jj - 1) * SUBLANES
        s1b = [st_ref[pl.ds(pl.multiple_of(h * 2 * PEER_NKEYS + a0, SUBLANES), SUBLANES), ls]
               for h in range(PEER_HEADS)]
        e1b = [e1_ref[pl.ds(pl.multiple_of(h * PEER_NKEYS + a0, SUBLANES), SUBLANES), ls]
               for h in range(PEER_HEADS)]
        taus = [tau_ref[h * SUBLANES:h * SUBLANES + 1, ls] for h in range(PEER_HEADS)]
        return s1b, e1b, taus

    def mixture_strip(slot, ls, rows, si, pin):
        s1b, e1b, taus = rows
        b0 = si * PEER_KEY_STRIP
        w = [jnp.zeros((PEER_KEY_STRIP, LANES), F32) for _ in range(groups)]
        for h in range(PEER_HEADS):
            r2 = h * 2 * PEER_NKEYS + PEER_NKEYS + b0
            s2 = st_ref[r2:r2 + PEER_KEY_STRIP, ls]
            e2 = e2_ref[h * PEER_NKEYS + b0:h * PEER_NKEYS + b0 + PEER_KEY_STRIP, ls]
            for ai in range(groups):
                keep = s1b[h][ai:ai + 1, :] + s2 >= taus[h]
                w[ai] = w[ai] + jnp.where(keep, e2 * e1b[h][ai:ai + 1, :], 0.0)
        for ai in range(groups):
            es = slice(ai * PEER_NKEYS + b0, ai * PEER_NKEYS + b0 + PEER_KEY_STRIP)
            h_t = hbuf[slot][es, ls]
            gelu = 0.5 * h_t * (1.0 + lax.erf(h_t * inv_sqrt2))
            val = w[ai] * gelu
            if ai == groups - 1:
                val = val + jnp.concatenate([pin] * (PEER_KEY_STRIP // SUBLANES), axis=0)
            act[slot][es, ls] = val.astype(BF16)

    def sweep(slot, do_hidden, do_mixture, do_output):
        n_strips = tiles_per_it * n_strips_tile
        half = n_strips // 2
        ch_h, ch_o = h_rows // half, o_rows // half

        def step(it, carry):
            h0 = pl.multiple_of(it * h_rows, h_rows)
            o0 = pl.multiple_of(it * o_rows, o_rows)
            r_h = r_o = None
            if do_hidden:
                r_h = jnp.dot(u_ref[pl.ds(h0, h_rows), :], xnt_ref[...], preferred_element_type=F32)
            if do_output:
                r_o = jnp.dot(vt_ref[pl.ds(o0, o_rows), :], act[1 - slot][...], preferred_element_type=F32)
            if not do_mixture:
                if do_hidden:
                    hbuf[1 - slot][pl.ds(h0, h_rows), :] = r_h
                if do_output:
                    acc[pl.ds(o0, o_rows), :] += r_o
                return carry
            for idx in range(n_strips):
                k, si = divmod(idx, n_strips_tile)
                ls = pl.ds(pl.multiple_of((it * tiles_per_it + k) * LANES, LANES), LANES)
                if si == 0:
                    rows = tile_rows(ls)
                if idx < half:
                    c = idx * ch_h
                    mixture_strip(slot, ls, rows, si, _zero_from(r_h[c:c + SUBLANES, 0:LANES]))
                    hbuf[1 - slot][pl.ds(pl.multiple_of(h0 + c, ch_h), ch_h), :] = r_h[c:c + ch_h, :]
                else:
                    c = (idx - half) * ch_o
                    mixture_strip(slot, ls, rows, si, _zero_from(r_o[c:c + SUBLANES, 0:LANES]))
                    acc[pl.ds(pl.multiple_of(o0 + c, ch_o), ch_o), :] += r_o[c:c + ch_o, :]
            return carry
        lax.fori_loop(0, n_it, step, 0)

    @pl.when(jj == 0)
    def _():
        acc[...] = jnp.zeros(acc.shape, F32)
        act[0][...] = jnp.zeros(act[0].shape, BF16)
        sweep(0, True, False, False)

    for slot in range(2):
        @pl.when((jj >= 1) & (jj <= n_blocks) & (jj % 2 == slot))
        def _(slot=slot):
            sweep(slot, True, True, True)

    @pl.when(jj == n_blocks + 1)
    def _():
        sweep((n_blocks + 1) % 2, False, False, True)
        y = x1_ref[...] + acc[...].T
        y_ref[...] = _rms(y, fg_ref[...])


def _peer_dense(xnt, x1, st, e1, e2, tau, u_bf, vt_bf, final_g, tn, eb):
    n = x1.shape[0]
    n_blocks = PEER_N // eb
    tok = lambda w: pl.BlockSpec((tn, w), lambda i, j: (i, 0))
    col = lambda a: pl.BlockSpec((a.shape[0], tn), lambda i, j: (0, i))
    return pl.pallas_call(
        functools.partial(_peer_dense_kernel, eb=eb, tn=tn, n_blocks=n_blocks),
        grid=(n // tn, n_blocks + 2),
        in_specs=[col(xnt), tok(D_MODEL), col(st), col(e1), col(e2), col(tau),
                  pl.BlockSpec((eb, D_MODEL), lambda i, j: (jnp.minimum(j, n_blocks - 1), 0)),
                  pl.BlockSpec((D_MODEL, eb), lambda i, j: (0, jnp.clip(j - 2, 0, n_blocks - 1))),
                  pl.BlockSpec((1, D_MODEL), lambda i, j: (0, 0))],
        out_specs=tok(D_MODEL),
        out_shape=jax.ShapeDtypeStruct((n, D_MODEL), F32),
        scratch_shapes=[pltpu.VMEM((D_MODEL, tn), F32), pltpu.VMEM((eb, tn), F32), pltpu.VMEM((eb, tn), F32),
                        pltpu.VMEM((eb, tn), BF16), pltpu.VMEM((eb, tn), BF16)],
        compiler_params=_cparams(("parallel", "arbitrary")),
        name="peer_dense",
    )(xnt, x1, st, e1, e2, tau, u_bf, vt_bf, final_g.reshape(1, -1))


def _group(x, s0, moba_fn, wts, tm, tn, eb):
    (n1, w_in, wa2, ba, gla_g, w_pa, w_pb, w_out, n2, wq, keys, u_bf, vt_bf, final_g) = wts
    bsz, t, _ = x.shape
    n = bsz * t
    x2 = x.reshape(n, D_MODEL)
    qa, ka, va, ra, la, qb, kb, vb, ga, gb = _in_proj(x2, n1, w_in, wa2, ba, tm)
    r3 = lambda a: a.reshape(bsz, t, a.shape[-1])
    oa, s_fin = _gla(r3(qa), r3(ka), r3(va), r3(la), r3(ra), s0, gla_g)
    ob = moba_fn(r3(qb), r3(kb), r3(vb))
    x1, xn, st = _merge(x2, oa.reshape(n, GLA_V_W), ob.reshape(n, MOBA_W), ga, gb,
                        w_pa, w_pb, w_out, n2, wq, keys, tm)
    e1, e2, tau = _peer_select(st, tn)
    y = _peer_dense(xn, x1, st, e1, e2, tau, u_bf, vt_bf, final_g, tn, eb)
    kv_shape = (1, bsz, t, MOBA_HEADS, MOBA_DH)
    return y.reshape(bsz, t, D_MODEL), kb.reshape(kv_shape), vb.reshape(kv_shape), s_fin[None]


def kernel(x_prompt, x_sample, cache_k, cache_v, state_gla, page_table, norm1_g, w_in, gla_wa2, gla_ba,
           gla_norm_g, w_pa, w_pb, w_out, norm2_g, peer_wq, peer_keys, peer_u, peer_v, final_g):
    assert w_in.shape[0] == 1, "single-layer step"
    u_bf = peer_u[0].astype(BF16)
    vt_bf = peer_v[0].astype(BF16).T
    wts = (norm1_g[0], w_in[0], gla_wa2[0], gla_ba[0], gla_norm_g[0], w_pa[0], w_pb[0], w_out[0],
           norm2_g[0], peer_wq[0], peer_keys[0], u_bf, vt_bf, final_g)
    bp = x_prompt.shape[0]
    s0p = jnp.zeros((bp, GLA_HEADS, GLA_DK, GLA_DV), F32)
    n_s = x_sample.shape[0] * x_sample.shape[1]
    tok_blk = lambda n, pref: pref if n % pref == 0 else n
    n_p = bp * x_prompt.shape[1]
    yp, kp, vp, sp = _group(x_prompt, s0p, _moba_prompt, wts,
                            tok_blk(n_p, 256), tok_blk(n_p, 512), PEER_EXPERT_BLOCK)
    ck, cv = cache_k[0], cache_v[0]
    ys, ks, vs, ss = _group(x_sample, state_gla[0],
                            lambda q, k, v: _moba_sample(q, k, v, ck, cv, page_table), wts,
                            tok_blk(n_s, 256), tok_blk(n_s, 512), PEER_EXPERT_BLOCK)
    return (yp, ys, kp, vp, sp, ks, vs, ss)
```
